```python
import jax
import jax.numpy as jnp
from jax import lax
import numpy as np

D_MODEL = 1024
BATCH = 4
SEQ = 4096
DEPTH = 2
DEC_BATCH = 128
DEC_SEQ = 8
PAST_LEN = 2048
PAGE_SIZE = 128

A_WINDOWS = (128, 512, 2048)
A_DILATIONS = (1, 4, 16)
A_GROUPS = 3
A_HEADS = D_MODEL // 128
A_DH = 64
A_W = A_HEADS * A_DH
A_BLK = 128
B_DH = 64
B_HEADS = D_MODEL // B_DH
B_W = B_HEADS * B_DH
B_W_LORA = 64
B_A_LORA = 64
B_G_LORA = 128
B_COLS = 3 * B_W + B_W_LORA + B_A_LORA + B_G_LORA
B_GN_EPS = 64e-5
C_DK = 64
C_DV = 64
C_HEADS = D_MODEL // 64
C_WK = C_HEADS * C_DK
C_WV = C_HEADS * C_DV
C_CHUNK = 64
C_LB_FLOOR = 1e-30
N_EXPERTS = 32
TOP_K = 4
D_FF = D_MODEL
SWIGLU_ALPHA = 1.702
SWIGLU_LIMIT = 7.0
MOE_BLOCK = 256
ROPE_THETA = 10000.0
LN_EPS = 1e-5
RMS_EPS = 1e-6
NEG_BIG = -1e30
DN_ALPHA = (2 * DEPTH) ** 0.25
DN_BETA = (8 * DEPTH) ** -0.25
IN_SIZES = (A_GROUPS * A_W, A_GROUPS * A_W, A_GROUPS * A_W, B_COLS, C_WK, C_WK, C_WV, C_WV, 3 * D_MODEL)
N_IN = sum(IN_SIZES)

kernel_name = 'dilated_rwkv7_hgrn2_gated_moe_step'


def _split(x, sizes):
    return jnp.split(x, [int(i) for i in np.cumsum(sizes)[:-1]], axis=-1)


def layer_norm(x, g, b):
    xf = x.astype(jnp.float32)
    mu = jnp.mean(xf, -1, keepdims=True)
    var = jnp.mean(jnp.square(xf - mu), -1, keepdims=True)
    return ((xf - mu) * lax.rsqrt(var + LN_EPS) * g + b).astype(x.dtype)


def rope(t, pos):
    half = t.shape[-1] // 2
    inv = ROPE_THETA ** (-jnp.arange(half, dtype=jnp.float32) / half)
    ang = pos.astype(jnp.float32)[:, None] * inv[None, :]
    cos = jnp.cos(ang)[None, :, None, :]
    sin = jnp.sin(ang)[None, :, None, :]
    tf = t.astype(jnp.float32)
    t1, t2 = tf[..., :half], tf[..., half:]
    return jnp.concatenate([t1 * cos - t2 * sin, t2 * cos + t1 * sin], -1).astype(t.dtype)


def dilated_attn_prompt(q, k, v, dil, window):
    n, L, H, dh = q.shape
    steps = window // dil
    res_len = -(-L // dil)
    nb = -(-res_len // A_BLK)
    pad = nb * A_BLK * dil - L

    def to_res(t):
        t = jnp.pad(t.astype(jnp.float32), ((0, 0), (0, pad), (0, 0), (0, 0)))
        t = t.reshape(n, nb * A_BLK, dil, H, dh).transpose(0, 2, 1, 3, 4)
        return t.reshape(n, dil, nb, A_BLK, H, dh)

    def with_prev(t):
        prev = jnp.pad(t[:, :, :-1], ((0, 0), (0, 0), (1, 0), (0, 0), (0, 0), (0, 0)))
        return jnp.concatenate([prev, t], axis=3)

    qr = to_res(q)
    kb = with_prev(to_res(k))
    vb = with_prev(to_res(v))
    s = jnp.einsum('bdnqhe,bdnkhe->bdnhqk', qr, kb) * (dh ** -0.5)
    q_i = jnp.arange(A_BLK)[:, None]
    k_i = jnp.arange(2 * A_BLK)[None, :] - A_BLK
    dist = q_i - k_i
    band = (dist >= 0) & (dist <= steps)
    mask = band[None] & ((jnp.arange(nb) > 0)[:, None, None] | (k_i >= 0)[None])
    s = jnp.where(mask[None, None, :, None], s, NEG_BIG)
    m = jnp.max(s, -1, keepdims=True)
    p = jnp.exp(s - m)
    l = jnp.sum(p, -1, keepdims=True)
    o = jnp.einsum('bdnhqk,bdnkhe->bdnqhe', p / l, vb)
    lse = jnp.swapaxes((m + jnp.log(l))[..., 0], 3, 4)
    o = o.reshape(n, dil, nb * A_BLK, H, dh).transpose(0, 2, 1, 3, 4).reshape(n, -1, H, dh)[:, :L]
    lse = lse.reshape(n, dil, nb * A_BLK, H).transpose(0, 2, 1, 3).reshape(n, -1, H)[:, :L]
    return o, lse


def dilated_attn_sample(q, k_all, v_all, dil, window):
    n, T, H, dh = q.shape
    buf_len = k_all.shape[1] - T
    n_k = window // dil + 1
    idx = buf_len + jnp.arange(T)[:, None] - dil * jnp.arange(n_k)[None, :]
    valid = idx >= 0
    idx = jnp.maximum(idx, 0)
    kg = k_all.astype(jnp.float32)[:, idx]
    vg = v_all.astype(jnp.float32)[:, idx]
    s = jnp.einsum('nthe,ntkhe->nthk', q.astype(jnp.float32), kg) * (dh ** -0.5)
    s = jnp.where(valid[None, :, None, :], s, NEG_BIG)
    m = jnp.max(s, -1, keepdims=True)
    p = jnp.exp(s - m)
    l = jnp.sum(p, -1, keepdims=True)
    o = jnp.einsum('nthk,ntkhe->nthe', p / l, vg)
    lse = (m + jnp.log(l))[..., 0]
    return o, lse


def rwkv7_scan(r, log_w, k, v, kk, a, s0):
    def step(s, xs):
        r_t, lw_t, k_t, v_t, kk_t, a_t = xs
        sa = jnp.einsum('nhvk,nhk->nhv', s, kk_t)
        s = (s * jnp.exp(lw_t)[:, :, None, :] - sa[..., None] * (kk_t * a_t)[:, :, None, :]
             + v_t[..., None] * k_t[:, :, None, :])
        return s, jnp.einsum('nhvk,nhk->nhv', s, r_t)

    xs = tuple(t.transpose(1, 0, 2, 3) for t in (r, log_w, k, v, kk, a))
    s, y = lax.scan(step, s0, xs)
    return y.transpose(1, 0, 2, 3), s


def hgrn2_chunked(q, k, i, log_f, s0):
    n, L, H, dk = q.shape
    dv = i.shape[-1]
    c = min(C_CHUNK, L)
    nc = -(-L // c)
    pad = nc * c - L

    def blocks(t):
        t = jnp.pad(t, ((0, 0), (0, pad), (0, 0), (0, 0)))
        return t.reshape(n, nc, c, H, t.shape[-1]).transpose(1, 0, 2, 3, 4)

    causal = jnp.tril(jnp.ones((c, c), bool))[None, :, :, None, None]

    def step(s, xs):
        qc, kc, ic, gc = xs
        b = jnp.cumsum(gc, axis=1)
        o_inter = jnp.einsum('nthk,nhkv->nthv', qc * jnp.exp(b), s)
        diff = b[:, :, None] - b[:, None, :]
        dec = jnp.where(causal, jnp.exp(jnp.minimum(diff, 0.0)), 0.0)
        att = jnp.einsum('ntshk,nshk->nhts', qc[:, :, None] * dec, kc)
        o_intra = jnp.einsum('nhts,nshv->nthv', att, ic)
        b_last = b[:, -1]
        s_new = (jnp.exp(b_last)[..., None] * s
                 + jnp.einsum('nshk,nshv->nhkv', kc * jnp.exp(b_last[:, None] - b), ic))
        return s_new, o_inter + o_intra

    s, o = lax.scan(step, s0, tuple(blocks(t) for t in (q, k, i, log_f)))
    o = o.transpose(1, 0, 2, 3, 4).reshape(n, nc * c, H, dv)[:, :L]
    return o, s


def moe_ffn(x2, w_router, b_router, w_up, b_up, w_down, b_down):
    T, d = x2.shape
    f32 = jnp.float32
    logits = jnp.dot(x2.astype(f32), w_router.astype(f32)) + b_router.astype(f32)
    top_val, top_idx = lax.top_k(logits, TOP_K)
    gate = jax.nn.softmax(top_val, axis=-1)
    n_assign = T * TOP_K
    flat_e = top_idx.reshape(n_assign).astype(jnp.int32)
    flat_tok = jnp.arange(n_assign, dtype=jnp.int32) // TOP_K
    flat_g = gate.reshape(n_assign)
    order = jnp.argsort(flat_e)
    e_sorted = flat_e[order]
    counts = jnp.zeros((N_EXPERTS,), jnp.int32).at[flat_e].add(1)
    padded = (counts + MOE_BLOCK - 1) // MOE_BLOCK * MOE_BLOCK
    start = jnp.cumsum(counts) - counts
    pend = jnp.cumsum(padded)
    pstart = pend - padded
    dest = pstart[e_sorted] + jnp.arange(n_assign, dtype=jnp.int32) - start[e_sorted]
    n_blocks = -(-n_assign // MOE_BLOCK) + N_EXPERTS
    n_slots = n_blocks * MOE_BLOCK
    slot_tok = jnp.full((n_slots,), T, jnp.int32).at[dest].set(flat_tok[order])
    slot_gate = jnp.zeros((n_slots,), f32).at[dest].set(flat_g[order])
    block_start = jnp.arange(n_blocks, dtype=jnp.int32) * MOE_BLOCK
    block_expert = jnp.minimum(jnp.searchsorted(pend, block_start, side='right'), N_EXPERTS - 1).astype(jnp.int32)
    x_pad = jnp.concatenate([x2, jnp.zeros((1, d), x2.dtype)], 0)

    def expert_block(args):
        toks, e = args
        h = (x_pad[toks] @ w_up[e] + b_up[e]).astype(f32)
        h_glu, h_lin = jnp.split(h, 2, axis=-1)
        h_glu = jnp.minimum(h_glu, SWIGLU_LIMIT)
        h_lin = jnp.clip(h_lin, -SWIGLU_LIMIT, SWIGLU_LIMIT)
        act = h_glu * jax.nn.sigmoid(SWIGLU_ALPHA * h_glu) * (h_lin + 1.0)
        return (act.astype(x2.dtype) @ w_down[e] + b_down[e]).astype(f32)

    out = lax.map(expert_block, (slot_tok.reshape(n_blocks, MOE_BLOCK), block_expert))
    y = jnp.zeros((T + 1, d), f32).at[slot_tok].add(out.reshape(n_slots, d) * slot_gate[:, None])
    return y[:T].astype(x2.dtype)


def mixer_sublayer(x, pos, a_bufs, b_prev, b_s0, c_s0, lb, lp):
    f32 = jnp.float32
    n, L, _ = x.shape
    proj = jnp.einsum('nld,dc->nlc', x, lp['w_in'])
    aq, ak, av, bcols, cq, cf, ci, cg, gates = _split(proj, IN_SIZES)

    shp = (n, L, A_GROUPS * A_HEADS, A_DH)
    aq = rope(aq.reshape(shp), pos).reshape(n, L, A_GROUPS, A_HEADS, A_DH)
    ak = rope(ak.reshape(shp), pos).reshape(n, L, A_GROUPS, A_HEADS, A_DH)
    av = av.reshape(n, L, A_GROUPS, A_HEADS, A_DH)
    outs, lses, a_new = [], [], []
    for g in range(A_GROUPS):
        qg, kg, vg = aq[:, :, g], ak[:, :, g], av[:, :, g]
        if a_bufs is None:
            o, lse = dilated_attn_prompt(qg, kg, vg, A_DILATIONS[g], A_WINDOWS[g])
            keep = min(A_WINDOWS[g], L)
            a_new.append(jnp.stack([kg[:, L - keep:], vg[:, L - keep:]], axis=2))
        else:
            buf = a_bufs[g]
            k_all = jnp.concatenate([buf[:, :, 0].astype(kg.dtype), kg], axis=1)
            v_all = jnp.concatenate([buf[:, :, 1].astype(vg.dtype), vg], axis=1)
            o, lse = dilated_attn_sample(qg, k_all, v_all, A_DILATIONS[g], A_WINDOWS[g])
            a_new.append(jnp.stack([kg, vg], axis=2))
        outs.append(o)
        lses.append(lse)
    wts = jax.nn.softmax(jnp.stack(lses), axis=0)
    o_a = jnp.sum(wts[..., None] * jnp.stack(outs), axis=0).reshape(n, L, A_W).astype(x.dtype)

    prev = jnp.concatenate([b_prev[:, None].astype(bcols.dtype), bcols[:, :-1]], axis=1)
    mixed = bcols + lp['b_mu'] * (prev - bcols)
    r, k, v, wl, al, gl = _split(mixed, (B_W, B_W, B_W, B_W_LORA, B_A_LORA, B_G_LORA))
    w_raw = -jax.nn.softplus(-(lp['b_w0'] + jnp.tanh(wl) @ lp['b_w2']).astype(f32)) - 0.5
    log_w = -jnp.exp(w_raw)
    a = jax.nn.sigmoid((lp['b_a0'] + al @ lp['b_a2']).astype(f32))
    gate_b = (jax.nn.sigmoid(gl) @ lp['b_g2']).astype(f32)
    hs = (n, L, B_HEADS, B_DH)
    kk = (k * lp['b_kk']).astype(f32).reshape(hs)
    kk = kk / jnp.maximum(jnp.sqrt(jnp.sum(kk * kk, -1, keepdims=True)), 1e-12)
    k_b = (k.astype(f32) * (1.0 + (a - 1.0) * lp['b_ka'])).reshape(hs)
    r_b = r.astype(f32).reshape(hs)
    v_b = v.astype(f32).reshape(hs)
    y_b, b_s = rwkv7_scan(r_b, log_w.reshape(hs), k_b, v_b, kk, a.reshape(hs), b_s0.astype(f32))
    mu = jnp.mean(y_b, -1, keepdims=True)
    var = jnp.mean(jnp.square(y_b - mu), -1, keepdims=True)
    y_b = ((y_b - mu) * lax.rsqrt(var + B_GN_EPS)).reshape(n, L, B_W) * lp['b_lnx_g'] + lp['b_lnx_b']
    bonus = (jnp.sum(r_b * k_b * lp['b_rk'], -1, keepdims=True) * v_b).reshape(n, L, B_W)
    o_b = ((y_b + bonus) * gate_b).astype(x.dtype)
    b_last = bcols[:, -1]

    cs = (n, L, C_HEADS, C_DK)
    z = cf.astype(f32)
    log_f = (jax.nn.log_sigmoid(z) + jax.nn.softplus(jnp.log(jnp.maximum(lb, C_LB_FLOOR)) - z)).reshape(cs)
    k_c = ((1.0 - lb) * jax.nn.sigmoid(-z)).reshape(cs)
    q_c = jax.nn.silu(cq.astype(f32)).reshape(cs)
    i_c = ci.astype(f32).reshape(n, L, C_HEADS, C_DV)
    o_c, c_s = hgrn2_chunked(q_c, k_c, i_c, log_f, c_s0.astype(f32))
    o_c = o_c * lax.rsqrt(jnp.mean(jnp.square(o_c), -1, keepdims=True) + RMS_EPS) * lp['c_norm_g']
    o_c = (o_c.reshape(n, L, C_WV) * jax.nn.silu(cg.astype(f32))).astype(x.dtype)

    g_a, g_b, g_c = jnp.split(jax.nn.sigmoid(gates), 3, axis=-1)
    merged = g_a * (o_a @ lp['w_br_a']) + g_b * (o_b @ lp['w_br_b']) + g_c * (o_c @ lp['w_br_c'])
    out = merged @ lp['w_out']
    return out, (a_new, b_last, b_s, c_s)


def hybrid_layer(x, pos, a_bufs, b_prev, b_s0, c_s0, lb, lp):
    mix, new_state = mixer_sublayer(x, pos, a_bufs, b_prev, b_s0, c_s0, lb, lp)
    x = layer_norm(DN_ALPHA * x + mix.astype(x.dtype), lp['ln_g'][0], lp['ln_b'][0])
    n, L, d = x.shape
    ffn = moe_ffn(x.reshape(n * L, d), lp['w_router'], lp['b_router'], lp['w_up'], lp['b_up'],
                  lp['w_down'], lp['b_down']).reshape(n, L, d)
    x = layer_norm(DN_ALPHA * x + ffn, lp['ln_g'][1], lp['ln_b'][1])
    return x, new_state


def setup_inputs(seed: int = 0) -> dict:
    key = jax.random.key(seed)
    keys = iter(jax.random.split(key, 48))

    def nrm(shape, scale):
        return jax.random.normal(next(keys), shape, jnp.float32) * scale

    return {
        'x_prompt': nrm((BATCH, SEQ, D_MODEL), 1.0),
        'x_sample': nrm((DEC_BATCH, DEC_SEQ, D_MODEL), 1.0),
        'cache_a0_kv': nrm((DEPTH, DEC_BATCH, min(A_WINDOWS[0], PAST_LEN), 2, A_HEADS, A_DH), 1.0),
        'cache_a1_kv': nrm((DEPTH, DEC_BATCH, min(A_WINDOWS[1], PAST_LEN), 2, A_HEADS, A_DH), 1.0),
        'cache_a2_kv': nrm((DEPTH, DEC_BATCH, min(A_WINDOWS[2], PAST_LEN), 2, A_HEADS, A_DH), 1.0),
        'state_b_shift': nrm((DEPTH, DEC_BATCH, B_COLS), 1.0),
        'state_b_wkv': nrm((DEPTH, DEC_BATCH, B_HEADS, B_DH, B_DH), 0.3),
        'state_c': nrm((DEPTH, DEC_BATCH, C_HEADS, C_DK, C_DV), 0.3),
        'w_in': nrm((DEPTH, D_MODEL, N_IN), D_MODEL ** -0.5),
        'b_mu': jax.random.uniform(next(keys), (DEPTH, B_COLS), jnp.float32),
        'b_w0': jnp.linspace(-6.0, -1.0, B_W, dtype=jnp.float32)[None] + nrm((DEPTH, B_W), 0.3),
        'b_w2': nrm((DEPTH, B_W_LORA, B_W), 0.5 * B_W_LORA ** -0.5),
        'b_a0': nrm((DEPTH, B_W), 0.1),
        'b_a2': nrm((DEPTH, B_A_LORA, B_W), B_A_LORA ** -0.5),
        'b_g2': nrm((DEPTH, B_G_LORA, B_W), B_G_LORA ** -0.5),
        'b_kk': 0.85 + nrm((DEPTH, B_W), 0.05),
        'b_ka': 1.0 + nrm((DEPTH, B_W), 0.05),
        'b_rk': nrm((DEPTH, B_HEADS, B_DH), 0.1),
        'b_lnx_g': 1.0 + nrm((DEPTH, B_W), 0.05),
        'b_lnx_b': nrm((DEPTH, B_W), 0.01),
        'c_lb': nrm((DEPTH, C_WK), 1.0),
        'c_norm_g': 1.0 + nrm((DEPTH, C_DV), 0.05),
        'w_br_a': nrm((DEPTH, A_W, D_MODEL), DN_BETA * A_W ** -0.5),
        'w_br_b': nrm((DEPTH, B_W, D_MODEL), DN_BETA * B_W ** -0.5),
        'w_br_c': nrm((DEPTH, C_WV, D_MODEL), DN_BETA * C_WV ** -0.5),
        'w_out': nrm((DEPTH, D_MODEL, D_MODEL), DN_BETA * D_MODEL ** -0.5),
        'ln_g': 1.0 + nrm((DEPTH, 2, D_MODEL), 0.05),
        'ln_b': nrm((DEPTH, 2, D_MODEL), 0.01),
        'w_router': nrm((DEPTH, D_MODEL, N_EXPERTS), D_MODEL ** -0.5),
        'b_router': nrm((DEPTH, N_EXPERTS), 0.01),
        'w_up': nrm((DEPTH, N_EXPERTS, D_MODEL, 2 * D_FF), D_MODEL ** -0.5),
        'b_up': nrm((DEPTH, N_EXPERTS, 2 * D_FF), 0.01),
        'w_down': nrm((DEPTH, N_EXPERTS, D_FF, D_MODEL), DN_BETA * D_FF ** -0.5),
        'b_down': nrm((DEPTH, N_EXPERTS, D_MODEL), 0.01),
    }


def reference(x_prompt, x_sample, cache_a0_kv, cache_a1_kv, cache_a2_kv, state_b_shift, state_b_wkv, state_c,
              w_in, b_mu, b_w0, b_w2, b_a0, b_a2, b_g2, b_kk, b_ka, b_rk, b_lnx_g, b_lnx_b, c_lb, c_norm_g,
              w_br_a, w_br_b, w_br_c, w_out, ln_g, ln_b, w_router, b_router, w_up, b_up, w_down, b_down):
    lb_all = jax.nn.softmax(c_lb.astype(jnp.float32), axis=0)
    lb_all = jnp.cumsum(lb_all, axis=0) - lb_all[0]

    def run_group(x, pos, a_caches, b_shift, b_wkv, c_st):
        new = ([], [], [], [], [], [])
        for l in range(DEPTH):
            lp = {'w_in': w_in[l], 'b_mu': b_mu[l], 'b_w0': b_w0[l], 'b_w2': b_w2[l], 'b_a0': b_a0[l],
                  'b_a2': b_a2[l], 'b_g2': b_g2[l], 'b_kk': b_kk[l], 'b_ka': b_ka[l], 'b_rk': b_rk[l],
                  'b_lnx_g': b_lnx_g[l], 'b_lnx_b': b_lnx_b[l], 'c_norm_g': c_norm_g[l],
                  'w_br_a': w_br_a[l], 'w_br_b': w_br_b[l], 'w_br_c': w_br_c[l], 'w_out': w_out[l],
                  'ln_g': ln_g[l], 'ln_b': ln_b[l], 'w_router': w_router[l], 'b_router': b_router[l],
                  'w_up': w_up[l], 'b_up': b_up[l], 'w_down': w_down[l], 'b_down': b_down[l]}
            bufs = None if a_caches is None else (a_caches[0][l], a_caches[1][l], a_caches[2][l])
            x, (a_new, b_last, b_s, c_s) = hybrid_layer(x, pos, bufs, b_shift[l], b_wkv[l], c_st[l],
                                                        lb_all[l], lp)
            for g in range(A_GROUPS):
                new[g].append(a_new[g])
            new[3].append(b_last)
            new[4].append(b_s.astype(x.dtype))
            new[5].append(c_s.astype(x.dtype))
        return x, [jnp.stack(s) for s in new]

    nb_p = x_prompt.shape[0]
    dt = x_prompt.dtype
    pos_p = jnp.arange(x_prompt.shape[1], dtype=jnp.int32)
    y_prompt, (p_a0, p_a1, p_a2, p_bs, p_bw, p_c) = run_group(
        x_prompt, pos_p, None,
        jnp.zeros((DEPTH, nb_p, B_COLS), dt),
        jnp.zeros((DEPTH, nb_p, B_HEADS, B_DH, B_DH), dt),
        jnp.zeros((DEPTH, nb_p, C_HEADS, C_DK, C_DV), dt))

    pos_s = PAST_LEN + jnp.arange(x_sample.shape[1], dtype=jnp.int32)
    y_sample, (s_a0, s_a1, s_a2, s_bs, s_bw, s_c) = run_group(
        x_sample, pos_s, (cache_a0_kv, cache_a1_kv, cache_a2_kv), state_b_shift, state_b_wkv, state_c)

    return (y_prompt, y_sample, p_a0, p_a1, p_a2, p_bs, p_bw, p_c, s_a0, s_a1, s_a2, s_bs, s_bw, s_c)
```

```python
import functools
import math

import numpy as np
import jax
import jax.numpy as jnp
from jax import lax
from jax.experimental import pallas as pl
from jax.experimental.pallas import tpu as pltpu

F32 = jnp.float32
BF16 = jnp.bfloat16

D_MODEL = 1024
DEPTH = 2
PAST_LEN = 2048
A_WINDOWS = (128, 512, 2048)
A_DILATIONS = (1, 4, 16)
A_GROUPS = 3
A_HEADS = 8
A_DH = 64
A_W = A_HEADS * A_DH
A_BLK = 128
B_DH = 64
B_HEADS = 16
B_W = 1024
B_W_LORA = 64
B_A_LORA = 64
B_G_LORA = 128
B_COLS = 3 * B_W + B_W_LORA + B_A_LORA + B_G_LORA
B_GN_EPS = 64e-5
C_DK = 64
C_DV = 64
C_HEADS = 16
C_W = 1024
C_LB_FLOOR = 1e-30
N_EXPERTS = 32
TOP_K = 4
D_FF = 1024
SWIGLU_ALPHA = 1.702
SWIGLU_LIMIT = 7.0
ROPE_THETA = 10000.0
LN_EPS = 1e-5
RMS_EPS = 1e-6
NEG_BIG = -1e30
DN_ALPHA = (2 * DEPTH) ** 0.25
QKV_W = 3 * A_GROUPS * A_W
OFF_B = QKV_W
OFF_C = OFF_B + B_COLS
OFF_G = OFF_C + 4 * C_W
N_IN = OFF_G + 3 * D_MODEL

LANES = 128
SUBLANES = 8
MXU_N = 256
VMEM_LIMIT = 56 * 1024 * 1024

RWKV_CHUNK = 64
HGRN_CHUNK = 64
MOE_BM = 256


def _cparams(sem):
    return pltpu.CompilerParams(dimension_semantics=sem, vmem_limit_bytes=VMEM_LIMIT)


def _bdot(a, b):
    return jnp.dot(a.astype(BF16), b.astype(BF16), preferred_element_type=F32)


def _bdot_nt(a, b):
    return lax.dot_general(a.astype(BF16), b.astype(BF16), (((1,), (1,)), ((), ())), preferred_element_type=F32)


def _bdot_tn(a, b):
    return lax.dot_general(a.astype(BF16), b.astype(BF16), (((0,), (0,)), ((), ())), preferred_element_type=F32)


def _split3(x):
    hi = x.astype(BF16)
    r1 = x - hi.astype(F32)
    mid = r1.astype(BF16)
    lo = (r1 - mid.astype(F32)).astype(BF16)
    return hi, mid, lo


def _dot_exact_lhs(sel, x):
    sel = sel.astype(BF16)
    hi, mid, lo = _split3(x)
    out = jnp.dot(sel, lo, preferred_element_type=F32)
    out = out + jnp.dot(sel, mid, preferred_element_type=F32)
    return out + jnp.dot(sel, hi, preferred_element_type=F32)


def _dot3(a, b):
    a_hi = a.astype(BF16)
    a_lo = (a - a_hi.astype(F32)).astype(BF16)
    b_hi = b.astype(BF16)
    b_lo = (b - b_hi.astype(F32)).astype(BF16)
    out = jnp.dot(a_lo, b_hi, preferred_element_type=F32)
    out = out + jnp.dot(a_hi, b_lo, preferred_element_type=F32)
    return out + jnp.dot(a_hi, b_hi, preferred_element_type=F32)


def _sigmoid(x):
    return 1.0 / (1.0 + jnp.exp(-x))


def _softplus(x):
    return jnp.maximum(x, 0.0) + jnp.log(1.0 + jnp.exp(-jnp.abs(x)))


def _proj_kernel(x_ref, w_ref, o_ref):
    o_ref[...] = jnp.dot(x_ref[...].astype(BF16), w_ref[...], preferred_element_type=F32)


def _project(x2, w_bf16, col_off, n_cols, tm):
    T, D = x2.shape
    assert col_off % MXU_N == 0 and n_cols % MXU_N == 0 and T % tm == 0
    off = col_off // MXU_N
    return pl.pallas_call(
        _proj_kernel,
        grid=(T // tm, n_cols // MXU_N),
        in_specs=[pl.BlockSpec((tm, D), lambda i, j: (i, 0)),
                  pl.BlockSpec((D, MXU_N), lambda i, j: (0, off + j))],
        out_specs=pl.BlockSpec((tm, MXU_N), lambda i, j: (i, j)),
        out_shape=jax.ShapeDtypeStruct((T, n_cols), F32),
        compiler_params=_cparams(("parallel", "arbitrary")),
    )(x2, w_bf16)


def _qkv_rope_kernel(x_ref, w_ref, cos_ref, sin_ref, o_ref):
    j = pl.program_id(1)
    acc = jnp.dot(x_ref[...].astype(BF16), w_ref[...], preferred_element_type=F32)

    @pl.when(j < 2 * A_GROUPS)
    def _():
        lane = lax.broadcasted_iota(jnp.int32, acc.shape, 1)
        first = (lane % A_DH) < (A_DH // 2)
        rot = jnp.where(first, pltpu.roll(acc, A_W - A_DH // 2, 1), pltpu.roll(acc, A_DH // 2, 1))
        o_ref[0] = acc * cos_ref[...] + rot * sin_ref[...]

    @pl.when(j >= 2 * A_GROUPS)
    def _():
        o_ref[0] = acc


def _project_qkv(x2, w_bf16, cos_t, sin_t, tm, period_blocks):
    T, D = x2.shape
    return pl.pallas_call(
        _qkv_rope_kernel,
        grid=(T // tm, 3 * A_GROUPS),
        in_specs=[pl.BlockSpec((tm, D), lambda i, j: (i, 0)),
                  pl.BlockSpec((D, A_W), lambda i, j: (0, j)),
                  pl.BlockSpec((tm, A_W), lambda i, j: (i % period_blocks, 0)),
                  pl.BlockSpec((tm, A_W), lambda i, j: (i % period_blocks, 0))],
        out_specs=pl.BlockSpec((1, tm, A_W), lambda i, j: (j, i, 0)),
        out_shape=jax.ShapeDtypeStruct((3 * A_GROUPS, T, A_W), F32),
        compiler_params=_cparams(("parallel", "arbitrary")),
    )(x2, w_bf16, cos_t, sin_t)


def _rope_tables(pos, rows):
    half = A_DH // 2
    inv = ROPE_THETA ** (-jnp.arange(half, dtype=F32) / half)
    ang = pos.astype(F32)[:, None] * inv[None, :]
    cos = jnp.cos(ang)
    sin = jnp.sin(ang)
    cos_h = jnp.concatenate([cos, cos], -1)
    sin_h = jnp.concatenate([-sin, sin], -1)
    cos_f = jnp.tile(cos_h, (rows // pos.shape[0], A_HEADS))
    sin_f = jnp.tile(sin_h, (rows // pos.shape[0], A_HEADS))
    return cos_f, sin_f


def _attn_prompt_kernel(q_ref, kc_ref, kp_ref, vc_ref, vp_ref, o_ref, lse_ref):
    jb = pl.program_id(2)
    two = 2 * A_BLK
    r2 = lax.broadcasted_iota(jnp.int32, (two, LANES), 0)
    l2 = lax.broadcasted_iota(jnp.int32, (two, LANES), 1)
    qi = r2 % A_BLK
    own_half = (r2 < A_BLK) == (l2 < A_DH)
    mask_c = l2 <= qi
    mask_p = (l2 >= qi) & (jb > 0)
    l1 = lax.broadcasted_iota(jnp.int32, (A_BLK, LANES), 1)
    first = l1 < A_DH
    for p in range(A_HEADS // 2):
        sl = slice(p * LANES, (p + 1) * LANES)
        q = q_ref[0, 0, :, sl] * (A_DH ** -0.5)
        q2 = jnp.where(own_half, jnp.concatenate([q, q], axis=0), 0.0)
        s_c = jnp.where(mask_c, _bdot_nt(q2, kc_ref[0, 0, :, sl]), NEG_BIG)
        s_p = jnp.where(mask_p, _bdot_nt(q2, kp_ref[0, 0, :, sl]), NEG_BIG)
        m = jnp.maximum(jnp.max(s_c, -1, keepdims=True), jnp.max(s_p, -1, keepdims=True))
        p_c = jnp.exp(s_c - m)
        p_p = jnp.exp(s_p - m)
        l = jnp.sum(p_c, -1, keepdims=True) + jnp.sum(p_p, -1, keepdims=True)
        o2 = (_bdot(p_c, vc_ref[0, 0, :, sl]) + _bdot(p_p, vp_ref[0, 0, :, sl])) / l
        lse2 = jnp.broadcast_to(m + jnp.log(l), (two, LANES))
        o_ref[0, 0, :, sl] = jnp.where(first, o2[:A_BLK], o2[A_BLK:])
        lse_ref[0, 0, :, sl] = jnp.where(first, lse2[:A_BLK], lse2[A_BLK:])


def _attn_prompt(q, k, v):
    n, d, R, _ = q.shape
    nb = R // A_BLK
    cur = pl.BlockSpec((1, 1, A_BLK, A_W), lambda b, r, j: (b, r, j, 0))
    prv = pl.BlockSpec((1, 1, A_BLK, A_W), lambda b, r, j: (b, r, jnp.maximum(j - 1, 0), 0))
    shp = jax.ShapeDtypeStruct((n, d, R, A_W), F32)
    return pl.pallas_call(
        _attn_prompt_kernel,
        grid=(n, d, nb),
        in_specs=[cur, cur, prv, cur, prv],
        out_specs=[cur, cur],
        out_shape=[shp, shp],
        compiler_params=_cparams(("parallel", "parallel", "arbitrary")),
    )(q, k, k, v, v)


def _attn_sample_kernel(qkv_ref, c0_ref, c1_ref, c2_ref, o_ref, *, n_new):
    caches = (c0_ref, c1_ref, c2_ref)
    n_res = A_WINDOWS[0] // A_DILATIONS[0]
    m_idx = lax.broadcasted_iota(jnp.int32, (n_res, A_HEADS, 1), 0)
    for i in range(n_new):
        outs, lses = [], []
        for g in range(A_GROUPS):
            d = A_DILATIONS[g]
            c_ref = caches[g]
            q = qkv_ref[g, 0, i] * (A_DH ** -0.5)
            kc = c_ref[:, i % d, 0]
            vc = c_ref[:, i % d, 1]
            s = jnp.sum(kc * q[None], axis=-1, keepdims=True)
            if i // d > 0:
                s = jnp.where(m_idx >= i // d, s, NEG_BIG)
            new_idx = list(range(i, -1, -d))
            s_new = [jnp.sum(qkv_ref[A_GROUPS + g, 0, i2] * q, axis=-1, keepdims=True) for i2 in new_idx]
            m = jnp.max(s, axis=0)
            for sn in s_new:
                m = jnp.maximum(m, sn)
            p = jnp.exp(s - m[None])
            l = jnp.sum(p, axis=0)
            o = jnp.sum(p * vc, axis=0)
            for sn, i2 in zip(s_new, new_idx):
                pn = jnp.exp(sn - m)
                l = l + pn
                o = o + pn * qkv_ref[2 * A_GROUPS + g, 0, i2]
            outs.append(o / l)
            lses.append(m + jnp.log(l))
        mx = jnp.maximum(jnp.maximum(lses[0], lses[1]), lses[2])
        ws = [jnp.exp(x - mx) for x in lses]
        den = ws[0] + ws[1] + ws[2]
        o_ref[0, i] = (ws[0] * outs[0] + ws[1] * outs[1] + ws[2] * outs[2]) / den


def _attn_sample(qkv5, caches, layer):
    _, N, T, H, E = qkv5.shape
    in_specs = [pl.BlockSpec((3 * A_GROUPS, 1, T, H, E), lambda n: (0, n, 0, 0, 0))]
    args = [qkv5]
    for g in range(A_GROUPS):
        d = A_DILATIONS[g]
        w = caches[g].shape[2]
        assert w == A_WINDOWS[g]
        used = min(d, T)
        args.append(caches[g].reshape(DEPTH, N, w // d, d, 2, H, E))
        in_specs.append(pl.BlockSpec((None, None, w // d, used, 2, H, E),
                                     lambda n: (layer, n, 0, 0, 0, 0, 0)))
    return pl.pallas_call(
        functools.partial(_attn_sample_kernel, n_new=T),
        grid=(N,),
        in_specs=in_specs,
        out_specs=pl.BlockSpec((1, T, H, E), lambda n: (n, 0, 0, 0)),
        out_shape=jax.ShapeDtypeStruct((N, T, H, E), F32),
        compiler_params=_cparams(("parallel",)),
    )(*args)


_DIMS = {"nn": (((1,), (0,)), ((), ())), "nt": (((1,), (1,)), ((), ())), "tn": (((0,), (0,)), ((), ()))}


def _mm(a, b, mode="nn"):
    dn = _DIMS[mode]
    a_hi = a.astype(BF16)
    a_lo = (a - a_hi.astype(F32)).astype(BF16)
    b_hi = b.astype(BF16)
    b_lo = (b - b_hi.astype(F32)).astype(BF16)
    out = lax.dot_general(a_lo, b_hi, dn, preferred_element_type=F32)
    out = out + lax.dot_general(a_hi, b_lo, dn, preferred_element_type=F32)
    return out + lax.dot_general(a_hi, b_hi, dn, preferred_element_type=F32)


def _tri_incl(c):
    t = lax.broadcasted_iota(jnp.int32, (c, c), 0)
    s = lax.broadcasted_iota(jnp.int32, (c, c), 1)
    return t, s


def _rwkv_kernel(bc_ref, bprev_ref, s0_ref, mu_ref, w0_ref, a0_ref, kkp_ref, ka_ref, w2_ref, a2_ref, g2_ref,
                 rk_ref, lng_ref, lnb_ref, o_ref, sout_ref,
                 prev_scr, s_scr, hr, hk, hv, hkk, ha, hb, hlw, hg, hy, *, C):
    c = pl.program_id(1)
    H, E = B_HEADS, B_DH

    @pl.when(c == 0)
    def _():
        prev_scr[...] = bprev_ref[0]
        s_scr[...] = s0_ref[0]

    bc = bc_ref[0]
    row = lax.broadcasted_iota(jnp.int32, bc.shape, 0)
    prev = jnp.where(row == 0, prev_scr[...], pltpu.roll(bc, 1, 0))
    prev_scr[...] = bc[C - 1:C, :]
    mixed = bc + mu_ref[...] * (prev - bc)
    r = mixed[:, 0:B_W]
    k = mixed[:, B_W:2 * B_W]
    v = mixed[:, 2 * B_W:3 * B_W]
    o1 = 3 * B_W
    wl = mixed[:, o1:o1 + B_W_LORA]
    al = mixed[:, o1 + B_W_LORA:o1 + B_W_LORA + B_A_LORA]
    gl = mixed[:, o1 + B_W_LORA + B_A_LORA:]
    w_raw = -_softplus(-(w0_ref[...] + _mm(jnp.tanh(wl), w2_ref[...]))) - 0.5
    lw = -jnp.exp(w_raw)
    a = _sigmoid(a0_ref[...] + _mm(al, a2_ref[...]))
    gate = _mm(_sigmoid(gl), g2_ref[...])
    kkr = k * kkp_ref[...]
    kb = k * (1.0 + (a - 1.0) * ka_ref[...])
    t_i, s_i = _tri_incl(C)
    b = _dot_exact_lhs(s_i <= t_i, lw)
    for h in range(H):
        sl = slice(h * E, (h + 1) * E)
        hr[h] = r[:, sl]
        hk[h] = kb[:, sl]
        hv[h] = v[:, sl]
        hkk[h] = kkr[:, sl]
        ha[h] = a[:, sl]
        hb[h] = b[:, sl]
        hlw[h] = lw[:, sl]
        hg[h] = gate[:, sl]

    strict = s_i < t_i
    incl = s_i <= t_i
    e_r = lax.broadcasted_iota(jnp.int32, (E, E), 0)
    e_c = lax.broadcasted_iota(jnp.int32, (E, E), 1)

    def head(h, carry):
        r_h, k_h, v_h, a_h, b_h = hr[h], hk[h], hv[h], ha[h], hb[h]
        kk = hkk[h]
        kk = kk / jnp.maximum(jnp.sqrt(jnp.sum(kk * kk, -1, keepdims=True)), 1e-12)
        eb = jnp.exp(b_h)
        ebp = jnp.exp(b_h - hlw[h])
        enb = jnp.exp(-b_h)
        gam = eb[C - 1:C, :]
        at = kk * ebp
        rt = r_h * eb
        bt = kk * a_h * enb
        kt = k_h * enb
        x = _mm(jnp.concatenate([at, rt], 0), jnp.concatenate([bt, kt], 0), "nt")
        lm = jnp.where(strict, x[:C, :C], 0.0)
        mk = jnp.where(strict, x[:C, C:], 0.0)
        arb = jnp.where(incl, x[C:, :C], 0.0)
        ark = jnp.where(incl, x[C:, C:], 0.0)
        z = jnp.concatenate([at, _mm(mk, v_h)], 1)
        for s in range(C - 1):
            z = z - lm[:, s:s + 1] * z[s:s + 1, :]
        qy = jnp.concatenate([rt, _mm(ark, v_h)], 1) - _mm(arb, z)
        top = jnp.where(e_r == e_c, gam, 0.0)
        gd = jnp.concatenate([top, _mm(v_h, kt * gam, "tn")], 0) - _mm(z, bt * gam, "tn")
        s_h = s_scr[h]
        y = _mm(qy[:, :E], s_h, "nt") + qy[:, E:]
        s_scr[h] = _mm(s_h, gd[:E]) + gd[E:]
        mu_y = jnp.mean(y, -1, keepdims=True)
        var = jnp.mean(jnp.square(y - mu_y), -1, keepdims=True)
        yn = (y - mu_y) * lax.rsqrt(var + B_GN_EPS) * lng_ref[h] + lnb_ref[h]
        bonus = jnp.sum(r_h * k_h * rk_ref[h], -1, keepdims=True) * v_h
        hy[h] = (yn + bonus) * hg[h]
        return carry

    lax.fori_loop(0, H, head, 0)
    for h in range(H):
        o_ref[0, :, h * E:(h + 1) * E] = hy[h]

    @pl.when(c == pl.num_programs(1) - 1)
    def _():
        sout_ref[0] = s_scr[...]


def _rwkv(bcols, b_prev, s0, p, C):
    N, L, _ = bcols.shape
    H, E = B_HEADS, B_DH
    row = lambda a: a.reshape(1, -1)
    per_head = lambda a: a.reshape(H, 1, E)
    full = lambda shape: pl.BlockSpec(shape, lambda n, c: (0,) * len(shape))
    hs = lambda: pltpu.VMEM((H, C, E), F32)
    return pl.pallas_call(
        functools.partial(_rwkv_kernel, C=C),
        grid=(N, L // C),
        in_specs=[pl.BlockSpec((1, C, B_COLS), lambda n, c: (n, c, 0)),
                  pl.BlockSpec((1, 1, B_COLS), lambda n, c: (n, 0, 0)),
                  pl.BlockSpec((1, H, E, E), lambda n, c: (n, 0, 0, 0)),
                  full((1, B_COLS)), full((1, B_W)), full((1, B_W)), full((1, B_W)), full((1, B_W)),
                  full((B_W_LORA, B_W)), full((B_A_LORA, B_W)), full((B_G_LORA, B_W)),
                  full((H, 1, E)), full((H, 1, E)), full((H, 1, E))],
        out_specs=[pl.BlockSpec((1, C, B_W), lambda n, c: (n, c, 0)),
                   pl.BlockSpec((1, H, E, E), lambda n, c: (n, 0, 0, 0))],
        out_shape=[jax.ShapeDtypeStruct((N, L, B_W), F32), jax.ShapeDtypeStruct((N, H, E, E), F32)],
        scratch_shapes=[pltpu.VMEM((1, B_COLS), F32), pltpu.VMEM((H, E, E), F32)] + [hs() for _ in range(9)],
        compiler_params=_cparams(("parallel", "arbitrary")),
    )(bcols, b_prev.reshape(N, 1, B_COLS), s0, row(p['b_mu']), row(p['b_w0']), row(p['b_a0']), row(p['b_kk']),
      row(p['b_ka']), p['b_w2'], p['b_a2'], p['b_g2'], per_head(p['b_rk']), per_head(p['b_lnx_g']),
      per_head(p['b_lnx_b']))


def _hgrn_levels(C):
    return int(math.log2(C))


def _hgrn_kernel(ch_ref, h0_ref, lb_ref, cng_ref, o_ref, hout_ref,
                 h_scr, hq, hk, hi, hb, hgt, hbp, ho, *, C):
    c = pl.program_id(1)
    H, E = C_HEADS, C_DK
    n_lv = _hgrn_levels(C)

    @pl.when(c == 0)
    def _():
        h_scr[...] = h0_ref[0]

    ch = ch_ref[0]
    cq = ch[:, 0:C_W]
    z = ch[:, C_W:2 * C_W]
    ci = ch[:, 2 * C_W:3 * C_W]
    cg = ch[:, 3 * C_W:4 * C_W]
    lb = lb_ref[...]
    loglb = jnp.log(jnp.maximum(lb, C_LB_FLOOR))
    log_f = -_softplus(-z) + _softplus(loglb - z)
    kc = (1.0 - lb) * _sigmoid(-z)
    qc = cq * _sigmoid(cq)
    gt = cg * _sigmoid(cg)
    t_i, s_i = _tri_incl(C)
    b = _dot_exact_lhs(s_i <= t_i, log_f)
    bps = []
    for lv in range(n_lv):
        m = 1 << lv
        sel = s_i == (t_i - (t_i % (2 * m)) + m - 1)
        bps.append(_dot_exact_lhs(sel, b))
    for h in range(H):
        sl = slice(h * E, (h + 1) * E)
        hq[h] = qc[:, sl]
        hk[h] = kc[:, sl]
        hi[h] = ci[:, sl]
        hb[h] = b[:, sl]
        hgt[h] = gt[:, sl]
        for lv in range(n_lv):
            hbp[lv * H + h] = bps[lv][:, sl]

    row_e = lax.broadcasted_iota(jnp.int32, (C, E), 0)
    e_r = lax.broadcasted_iota(jnp.int32, (E, E), 0)
    e_c = lax.broadcasted_iota(jnp.int32, (E, E), 1)

    def head(h, carry):
        q_h, k_h, i_h, b_h = hq[h], hk[h], hi[h], hb[h]
        att = jnp.where(t_i == s_i, _mm(q_h, k_h, "nt"), 0.0)
        for lv in range(n_lv):
            m = 1 << lv
            bp = hbp[lv * H + h]
            upper = (row_e % (2 * m)) >= m
            qm = jnp.where(upper, q_h * jnp.exp(jnp.minimum(b_h - bp, 0.0)), 0.0)
            km = jnp.where(upper, 0.0, k_h * jnp.exp(jnp.minimum(bp - b_h, 0.0)))
            same = (t_i // (2 * m)) == (s_i // (2 * m))
            att = att + jnp.where(same, _mm(qm, km, "nt"), 0.0)
        h_m = h_scr[h]
        o = _mm(q_h * jnp.exp(b_h), h_m) + _mm(att, i_h)
        b_last = b_h[C - 1:C, :]
        dg = jnp.where(e_r == e_c, jnp.exp(b_last), 0.0)
        kdec = k_h * jnp.exp(b_last - b_h)
        h_scr[h] = _mm(jnp.concatenate([dg, kdec], 0), jnp.concatenate([h_m, i_h], 0), "tn")
        o = o * lax.rsqrt(jnp.mean(jnp.square(o), -1, keepdims=True) + RMS_EPS) * cng_ref[...]
        ho[h] = o * hgt[h]
        return carry

    lax.fori_loop(0, H, head, 0)
    for h in range(H):
        o_ref[0, :, h * E:(h + 1) * E] = ho[h]

    @pl.when(c == pl.num_programs(1) - 1)
    def _():
        hout_ref[0] = h_scr[...]


def _hgrn(ch, h0, lb, c_norm_g, C):
    N, L, _ = ch.shape
    H, E = C_HEADS, C_DK
    n_lv = _hgrn_levels(C)
    full = lambda shape: pl.BlockSpec(shape, lambda n, c: (0,) * len(shape))
    hs = lambda k=1: pltpu.VMEM((k * H, C, E), F32)
    return pl.pallas_call(
        functools.partial(_hgrn_kernel, C=C),
        grid=(N, L // C),
        in_specs=[pl.BlockSpec((1, C, 4 * C_W), lambda n, c: (n, c, 0)),
                  pl.BlockSpec((1, H, E, E), lambda n, c: (n, 0, 0, 0)),
                  full((1, C_W)), full((1, E))],
        out_specs=[pl.BlockSpec((1, C, C_W), lambda n, c: (n, c, 0)),
                   pl.BlockSpec((1, H, E, E), lambda n, c: (n, 0, 0, 0))],
        out_shape=[jax.ShapeDtypeStruct((N, L, C_W), F32), jax.ShapeDtypeStruct((N, H, E, E), F32)],
        scratch_shapes=[pltpu.VMEM((H, E, E), F32), hs(), hs(), hs(), hs(), hs(), hs(n_lv), hs()],
        compiler_params=_cparams(("parallel", "arbitrary")),
    )(ch, h0, lb.reshape(1, C_W), c_norm_g.reshape(1, E))


def _layer_norm(v, g, b):
    mu = jnp.mean(v, -1, keepdims=True)
    var = jnp.mean(jnp.square(v - mu), -1, keepdims=True)
    return (v - mu) * lax.rsqrt(var + LN_EPS) * g + b


def _merge_kernel(*refs, n_attn, tm):
    attn = refs[:n_attn]
    (ob_ref, oc_ref, gates_ref, x_ref, wa_ref, wb_ref, wc_ref, wo_ref, lng_ref, lnb_ref, wr_ref, br_ref,
     x1_ref, idx_ref, gate_ref, rank_ref, cnt_ref, run_scr) = refs[n_attn:]
    i = pl.program_id(0)

    @pl.when(i == 0)
    def _():
        run_scr[...] = jnp.zeros_like(run_scr)

    if n_attn == 1:
        o_a = attn[0][...]
    else:
        o0, l0, o1, l1, o2, l2 = (r[...] for r in attn)
        mx = jnp.maximum(jnp.maximum(l0, l1), l2)
        w0, w1, w2 = jnp.exp(l0 - mx), jnp.exp(l1 - mx), jnp.exp(l2 - mx)
        o_a = (w0 * o0 + w1 * o1 + w2 * o2) / (w0 + w1 + w2)
    gates = gates_ref[...]
    merged = (_sigmoid(gates[:, 0:D_MODEL]) * _bdot(o_a, wa_ref[...])
              + _sigmoid(gates[:, D_MODEL:2 * D_MODEL]) * _bdot(ob_ref[...], wb_ref[...])
              + _sigmoid(gates[:, 2 * D_MODEL:]) * _bdot(oc_ref[...], wc_ref[...]))
    mix = _bdot(merged, wo_ref[...])
    x1 = _layer_norm(DN_ALPHA * x_ref[...] + mix, lng_ref[...], lnb_ref[...])
    x1_ref[...] = x1

    logits = _mm(x1, wr_ref[...]) + br_ref[...]
    lane = lax.broadcasted_iota(jnp.int32, (tm, LANES), 1)
    cur = jnp.where(lane < N_EXPERTS, logits, -jnp.inf)
    vals, idxs, sels = [], [], []
    for _ in range(TOP_K):
        mx = jnp.max(cur, -1, keepdims=True)
        idx = jnp.min(jnp.where(cur == mx, lane, LANES), -1, keepdims=True)
        sel = lane == idx
        vals.append(mx)
        idxs.append(idx)
        sels.append(sel)
        cur = jnp.where(sel, -jnp.inf, cur)
    es = [jnp.exp(v - vals[0]) for v in vals]
    den = es[0] + es[1] + es[2] + es[3]
    mask = jnp.zeros((tm, LANES), F32)
    for sel in sels:
        mask = jnp.where(sel, 1.0, mask)
    r_i = lax.broadcasted_iota(jnp.int32, (tm, tm), 0)
    c_i = lax.broadcasted_iota(jnp.int32, (tm, tm), 1)
    excl = jnp.dot((c_i < r_i).astype(BF16), mask.astype(BF16), preferred_element_type=F32) + run_scr[...]
    idx_o = jnp.zeros((tm, LANES), jnp.int32)
    gate_o = jnp.zeros((tm, LANES), F32)
    rank_o = jnp.zeros((tm, LANES), F32)
    for j in range(TOP_K):
        rank_j = jnp.sum(jnp.where(sels[j], excl, 0.0), -1, keepdims=True)
        idx_o = jnp.where(lane == j, idxs[j], idx_o)
        gate_o = jnp.where(lane == j, es[j] / den, gate_o)
        rank_o = jnp.where(lane == j, rank_j, rank_o)
    idx_ref[...] = idx_o
    gate_ref[...] = gate_o
    rank_ref[...] = rank_o.astype(jnp.int32)
    run_scr[...] = run_scr[...] + jnp.sum(mask, axis=0, keepdims=True)
    cnt_ref[...] = jnp.broadcast_to(run_scr[...], cnt_ref.shape)


def _merge(attn, o_b, o_c, gates, x2, wts, tm):
    T = x2.shape[0]
    n_attn = len(attn)
    rowspec = lambda w: pl.BlockSpec((tm, w), lambda i: (i, 0))
    full = lambda a: pl.BlockSpec(a.shape, lambda i: (0,) * a.ndim)
    wlist = [wts['w_br_a'], wts['w_br_b'], wts['w_br_c'], wts['w_out'], wts['ln_g0'], wts['ln_b0'],
             wts['w_router'], wts['b_router']]
    return pl.pallas_call(
        functools.partial(_merge_kernel, n_attn=n_attn, tm=tm),
        grid=(T // tm,),
        in_specs=[rowspec(A_W)] * n_attn + [rowspec(B_W), rowspec(C_W), rowspec(3 * D_MODEL), rowspec(D_MODEL)]
                 + [full(w) for w in wlist],
        out_specs=[rowspec(D_MODEL), rowspec(LANES), rowspec(LANES), rowspec(LANES),
                   pl.BlockSpec((SUBLANES, LANES), lambda i: (0, 0))],
        out_shape=[jax.ShapeDtypeStruct((T, D_MODEL), F32), jax.ShapeDtypeStruct((T, LANES), jnp.int32),
                   jax.ShapeDtypeStruct((T, LANES), F32), jax.ShapeDtypeStruct((T, LANES), jnp.int32),
                   jax.ShapeDtypeStruct((SUBLANES, LANES), F32)],
        scratch_shapes=[pltpu.VMEM((1, LANES), F32)],
        compiler_params=_cparams(("arbitrary",)),
    )(*attn, o_b, o_c, gates, x2, *wlist)


def _dispatch_kernel(dest_ref, x_ref, xs_in_ref, xs_ref, sem, *, td):
    del xs_in_ref

    def row_copy(k):
        return pltpu.make_async_copy(x_ref.at[pl.ds(k // TOP_K, 1)], xs_ref.at[pl.ds(dest_ref[0, 0, k], 1)], sem)

    def issue(k, carry):
        row_copy(k).start()
        return carry

    def drain(k, carry):
        row_copy(k).wait()
        return carry

    lax.fori_loop(0, td * TOP_K, issue, 0)
    lax.fori_loop(0, td * TOP_K, drain, 0)


def _dispatch(x1, dest, n_slots, td):
    T = x1.shape[0]
    zeros = jnp.zeros((n_slots, D_MODEL), F32)
    return pl.pallas_call(
        functools.partial(_dispatch_kernel, td=td),
        grid=(T // td,),
        in_specs=[pl.BlockSpec((1, 1, td * TOP_K), lambda i: (i, 0, 0), memory_space=pltpu.SMEM),
                  pl.BlockSpec((td, D_MODEL), lambda i: (i, 0)),
                  pl.BlockSpec(memory_space=pl.ANY)],
        out_specs=pl.BlockSpec(memory_space=pl.ANY),
        out_shape=jax.ShapeDtypeStruct((n_slots, D_MODEL), F32),
        scratch_shapes=[pltpu.SemaphoreType.DMA(())],
        input_output_aliases={2: 0},
        compiler_params=_cparams(("arbitrary",)),
    )(dest.reshape(T // td, 1, td * TOP_K), x1, zeros)


def _expert_kernel(be_ref, nu_ref, xs_ref, wu_ref, bu_ref, wd_ref, bd_ref, o_ref):
    b = pl.program_id(0)

    @pl.when(b < nu_ref[0])
    def _():
        h = jnp.dot(xs_ref[...].astype(BF16), wu_ref[0], preferred_element_type=F32) + bu_ref[0]
        h_glu = jnp.minimum(h[:, :D_FF], SWIGLU_LIMIT)
        h_lin = jnp.clip(h[:, D_FF:], -SWIGLU_LIMIT, SWIGLU_LIMIT)
        act = h_glu * _sigmoid(SWIGLU_ALPHA * h_glu) * (h_lin + 1.0)
        o_ref[...] = jnp.dot(act.astype(BF16), wd_ref[0], preferred_element_type=F32) + bd_ref[0]

    @pl.when(b >= nu_ref[0])
    def _():
        o_ref[...] = jnp.zeros_like(o_ref)


def _experts(xs, block_expert, n_used, w_up, b_up, w_down, b_down):
    n_slots = xs.shape[0]
    n_blocks = n_slots // MOE_BM
    grid_spec = pltpu.PrefetchScalarGridSpec(
        num_scalar_prefetch=2,
        grid=(n_blocks,),
        in_specs=[pl.BlockSpec((MOE_BM, D_MODEL), lambda b, be, nu: (b, 0)),
                  pl.BlockSpec((1, D_MODEL, 2 * D_FF), lambda b, be, nu: (be[b], 0, 0)),
                  pl.BlockSpec((1, 1, 2 * D_FF), lambda b, be, nu: (be[b], 0, 0)),
                  pl.BlockSpec((1, D_FF, D_MODEL), lambda b, be, nu: (be[b], 0, 0)),
                  pl.BlockSpec((1, 1, D_MODEL), lambda b, be, nu: (be[b], 0, 0))],
        out_specs=pl.BlockSpec((MOE_BM, D_MODEL), lambda b, be, nu: (b, 0)),
    )
    return pl.pallas_call(
        _expert_kernel,
        grid_spec=grid_spec,
        out_shape=jax.ShapeDtypeStruct((n_slots, D_MODEL), F32),
        compiler_params=_cparams(("arbitrary",)),
    )(block_expert, n_used, xs, w_up, b_up.reshape(N_EXPERTS, 1, 2 * D_FF), w_down,
      b_down.reshape(N_EXPERTS, 1, D_MODEL))


def _combine_kernel(dest_ref, gate_ref, x1_ref, eo_ref, lng_ref, lnb_ref, o_ref, buf, sem, *, tc):
    def row_copy(k):
        return pltpu.make_async_copy(eo_ref.at[pl.ds(dest_ref[0, 0, k], 1)],
                                     buf.at[k % TOP_K, pl.ds(k // TOP_K, 1)], sem)

    def issue(k, carry):
        row_copy(k).start()
        return carry

    def drain(k, carry):
        row_copy(k).wait()
        return carry

    lax.fori_loop(0, tc * TOP_K, issue, 0)
    lax.fori_loop(0, tc * TOP_K, drain, 0)
    gate = gate_ref[...]
    y = gate[:, 0:1] * buf[0]
    for j in range(1, TOP_K):
        y = y + gate[:, j:j + 1] * buf[j]
    o_ref[...] = _layer_norm(DN_ALPHA * x1_ref[...] + y, lng_ref[...], lnb_ref[...])


def _combine(dest, gate, x1, eo, ln_g, ln_b, tc):
    T = x1.shape[0]
    full = lambda a: pl.BlockSpec(a.shape, lambda i: (0,) * a.ndim)
    return pl.pallas_call(
        functools.partial(_combine_kernel, tc=tc),
        grid=(T // tc,),
        in_specs=[pl.BlockSpec((1, 1, tc * TOP_K), lambda i: (i, 0, 0), memory_space=pltpu.SMEM),
                  pl.BlockSpec((tc, LANES), lambda i: (i, 0)),
                  pl.BlockSpec((tc, D_MODEL), lambda i: (i, 0)),
                  pl.BlockSpec(memory_space=pl.ANY), full(ln_g), full(ln_b)],
        out_specs=pl.BlockSpec((tc, D_MODEL), lambda i: (i, 0)),
        out_shape=jax.ShapeDtypeStruct((T, D_MODEL), F32),
        scratch_shapes=[pltpu.VMEM((TOP_K, tc, D_MODEL), F32), pltpu.SemaphoreType.DMA(())],
        compiler_params=_cparams(("arbitrary",)),
    )(dest.reshape(T // tc, 1, tc * TOP_K), gate, x1, eo, ln_g, ln_b)


def _moe(x1, top_idx, top_gate, top_rank, counts, wts):
    T = x1.shape[0]
    cnt = counts[0, :N_EXPERTS].astype(jnp.int32)
    padded = (cnt + MOE_BM - 1) // MOE_BM * MOE_BM
    pend = jnp.cumsum(padded)
    pstart = pend - padded
    dest = (pstart[top_idx[:, :TOP_K]] + top_rank[:, :TOP_K]).astype(jnp.int32)
    n_blocks = T * TOP_K // MOE_BM + N_EXPERTS
    block_start = jnp.arange(n_blocks, dtype=jnp.int32) * MOE_BM
    block_expert = jnp.minimum(jnp.searchsorted(pend, block_start, side='right'), N_EXPERTS - 1).astype(jnp.int32)
    n_used = (pend[-1:] // MOE_BM).astype(jnp.int32)
    xs = _dispatch(x1, dest, n_blocks * MOE_BM, min(256, T))
    eo = _experts(xs, block_expert, n_used, wts['w_up'], wts['b_up'], wts['w_down'], wts['b_down'])
    return _combine(dest, top_gate, x1, eo, wts['ln_g1'], wts['ln_b1'], min(128, T))


def _hybrid_layer(x, pos, caches, layer, b_shift, b_wkv, c_st, lb, wts):
    N, L, D = x.shape
    T = N * L
    x2 = x.reshape(T, D)
    tm = min(1024, T)
    rows = max(L, tm)
    cos_t, sin_t = _rope_tables(pos, rows)
    qkv = _project_qkv(x2, wts['w_in'], cos_t, sin_t, tm, rows // tm)
    bcols = _project(x2, wts['w_in'], OFF_B, B_COLS, tm).reshape(N, L, B_COLS)
    ch = _project(x2, wts['w_in'], OFF_C, 4 * C_W, tm).reshape(N, L, 4 * C_W)
    gates = _project(x2, wts['w_in'], OFF_G, 3 * D_MODEL, tm)

    a_new = []
    if caches is None:
        attn = []
        for g in range(A_GROUPS):
            d = A_DILATIONS[g]
            to_res = lambda t: t.reshape(N, L // d, d, A_W).transpose(0, 2, 1, 3)
            o, lse = _attn_prompt(to_res(qkv[g]), to_res(qkv[A_GROUPS + g]), to_res(qkv[2 * A_GROUPS + g]))
            from_res = lambda t: t.transpose(0, 2, 1, 3).reshape(T, A_W)
            attn += [from_res(o), from_res(lse)]
            keep = min(A_WINDOWS[g], L)
            kg = qkv[A_GROUPS + g].reshape(N, L, A_HEADS, A_DH)[:, L - keep:]
            vg = qkv[2 * A_GROUPS + g].reshape(N, L, A_HEADS, A_DH)[:, L - keep:]
            a_new.append(jnp.stack([kg, vg], axis=2))
    else:
        qkv5 = qkv.reshape(3 * A_GROUPS, N, L, A_HEADS, A_DH)
        attn = [_attn_sample(qkv5, caches, layer).reshape(T, A_W)]
        for g in range(A_GROUPS):
            a_new.append(jnp.stack([qkv5[A_GROUPS + g], qkv5[2 * A_GROUPS + g]], axis=2))

    o_b, b_s = _rwkv(bcols, b_shift, b_wkv, wts, min(RWKV_CHUNK, L))
    o_c, c_s = _hgrn(ch, c_st, lb, wts['c_norm_g'], min(HGRN_CHUNK, L))
    x1, top_idx, top_gate, top_rank, counts = _merge(
        attn, o_b.reshape(T, B_W), o_c.reshape(T, C_W), gates, x2, wts, min(512, T))
    x_out = _moe(x1, top_idx, top_gate, top_rank, counts, wts)
    return x_out.reshape(N, L, D), (a_new, bcols[:, -1], b_s, c_s)


def kernel(x_prompt, x_sample, cache_a0_kv, cache_a1_kv, cache_a2_kv, state_b_shift, state_b_wkv, state_c,
           w_in, b_mu, b_w0, b_w2, b_a0, b_a2, b_g2, b_kk, b_ka, b_rk, b_lnx_g, b_lnx_b, c_lb, c_norm_g,
           w_br_a, w_br_b, w_br_c, w_out, ln_g, ln_b, w_router, b_router, w_up, b_up, w_down, b_down):
    lb_all = jax.nn.softmax(c_lb.astype(F32), axis=0)
    lb_all = jnp.cumsum(lb_all, axis=0) - lb_all[0]
    pad_e = LANES - N_EXPERTS
    layers = []
    for l in range(DEPTH):
        layers.append({
            'w_in': w_in[l].astype(BF16), 'b_mu': b_mu[l], 'b_w0': b_w0[l], 'b_w2': b_w2[l], 'b_a0': b_a0[l],
            'b_a2': b_a2[l], 'b_g2': b_g2[l], 'b_kk': b_kk[l], 'b_ka': b_ka[l], 'b_rk': b_rk[l],
            'b_lnx_g': b_lnx_g[l], 'b_lnx_b': b_lnx_b[l], 'c_norm_g': c_norm_g[l],
            'w_br_a': w_br_a[l].astype(BF16), 'w_br_b': w_br_b[l].astype(BF16), 'w_br_c': w_br_c[l].astype(BF16),
            'w_out': w_out[l].astype(BF16),
            'ln_g0': ln_g[l, 0].reshape(1, D_MODEL), 'ln_b0': ln_b[l, 0].reshape(1, D_MODEL),
            'ln_g1': ln_g[l, 1].reshape(1, D_MODEL), 'ln_b1': ln_b[l, 1].reshape(1, D_MODEL),
            'w_router': jnp.pad(w_router[l].astype(F32), ((0, 0), (0, pad_e))),
            'b_router': jnp.pad(b_router[l].astype(F32), (0, pad_e)).reshape(1, LANES),
            'w_up': w_up[l].astype(BF16), 'b_up': b_up[l], 'w_down': w_down[l].astype(BF16), 'b_down': b_down[l]})

    def run_group(x, pos, caches, b_shift, b_wkv, c_st):
        new = ([], [], [], [], [], [])
        for l in range(DEPTH):
            x, (a_new, b_last, b_s, c_s) = _hybrid_layer(x, pos, caches, l, b_shift[l], b_wkv[l], c_st[l],
                                                         lb_all[l], layers[l])
            for g in range(A_GROUPS):
                new[g].append(a_new[g])
            new[3].append(b_last)
            new[4].append(b_s)
            new[5].append(c_s)
        return x, [jnp.stack(s) for s in new]

    nb_p, seq = x_prompt.shape[0], x_prompt.shape[1]
    y_prompt, (p_a0, p_a1, p_a2, p_bs, p_bw, p_c) = run_group(
        x_prompt, jnp.arange(seq, dtype=jnp.int32), None,
        jnp.zeros((DEPTH, nb_p, B_COLS), F32),
        jnp.zeros((DEPTH, nb_p, B_HEADS, B_DH, B_DH), F32),
        jnp.zeros((DEPTH, nb_p, C_HEADS, C_DK, C_DV), F32))
    pos_s = PAST_LEN + jnp.arange(x_sample.shape[1], dtype=jnp.int32)
    y_sample, (s_a0, s_a1, s_a2, s_bs, s_bw, s_c) = run_group(
        x_sample, pos_s, (cache_a0_kv, cache_a1_kv, cache_a2_kv), state_b_shift, state_b_wkv, state_c)
    return (y_prompt, y_sample, p_a0, p_a1, p_a2, p_bs, p_bw, p_c, s_a0, s_a1, s_a2, s_bs, s_bw, s_c)
```

```python
import functools
import math

import numpy as np
import jax
import jax.numpy as jnp
from jax import lax
from jax.experimental import pallas as pl
from jax.experimental.pallas import tpu as pltpu

F32 = jnp.float32
BF16 = jnp.bfloat16

D_MODEL = 1024
DEPTH = 2
PAST_LEN = 2048
A_WINDOWS = (128, 512, 2048)
A_DILATIONS = (1, 4, 16)
A_GROUPS = 3
A_HEADS = 8
A_DH = 64
A_W = A_HEADS * A_DH
A_BLK = 128
B_DH = 64
B_HEADS = 16
B_W = 1024
B_W_LORA = 64
B_A_LORA = 64
B_G_LORA = 128
B_COLS = 3 * B_W + B_W_LORA + B_A_LORA + B_G_LORA
B_GN_EPS = 64e-5
C_DK = 64
C_DV = 64
C_HEADS = 16
C_W = 1024
C_LB_FLOOR = 1e-30
N_EXPERTS = 32
TOP_K = 4
D_FF = 1024
SWIGLU_ALPHA = 1.702
SWIGLU_LIMIT = 7.0
ROPE_THETA = 10000.0
LN_EPS = 1e-5
RMS_EPS = 1e-6
NEG_BIG = -1e30
DN_ALPHA = (2 * DEPTH) ** 0.25
QKV_W = 3 * A_GROUPS * A_W
OFF_B = QKV_W
OFF_C = OFF_B + B_COLS
OFF_G = OFF_C + 4 * C_W
N_IN = OFF_G + 3 * D_MODEL

LANES = 128
SUBLANES = 8
MXU_N = 256
VMEM_LIMIT = 56 * 1024 * 1024

RWKV_CHUNK = 64
HGRN_CHUNK = 64
MOE_BM = 256


def _cparams(sem):
    return pltpu.CompilerParams(dimension_semantics=sem, vmem_limit_bytes=VMEM_LIMIT)


def _bdot(a, b):
    return jnp.dot(a.astype(BF16), b.astype(BF16), preferred_element_type=F32)


def _bdot_nt(a, b):
    return lax.dot_general(a.astype(BF16), b.astype(BF16), (((1,), (1,)), ((), ())), preferred_element_type=F32)


def _bdot_tn(a, b):
    return lax.dot_general(a.astype(BF16), b.astype(BF16), (((0,), (0,)), ((), ())), preferred_element_type=F32)


def _split3(x):
    hi = x.astype(BF16)
    r1 = x - hi.astype(F32)
    mid = r1.astype(BF16)
    lo = (r1 - mid.astype(F32)).astype(BF16)
    return hi, mid, lo


def _dot_exact_lhs(sel, x):
    sel = sel.astype(BF16)
    hi, mid, lo = _split3(x)
    out = jnp.dot(sel, lo, preferred_element_type=F32)
    out = out + jnp.dot(sel, mid, preferred_element_type=F32)
    return out + jnp.dot(sel, hi, preferred_element_type=F32)


def _dot3(a, b):
    a_hi = a.astype(BF16)
    a_lo = (a - a_hi.astype(F32)).astype(BF16)
    b_hi = b.astype(BF16)
    b_lo = (b - b_hi.astype(F32)).astype(BF16)
    out = jnp.dot(a_lo, b_hi, preferred_element_type=F32)
    out = out + jnp.dot(a_hi, b_lo, preferred_element_type=F32)
    return out + jnp.dot(a_hi, b_hi, preferred_element_type=F32)


def _sigmoid(x):
    return 1.0 / (1.0 + jnp.exp(-x))


def _softplus(x):
    return jnp.maximum(x, 0.0) + jnp.log(1.0 + jnp.exp(-jnp.abs(x)))


def _proj_kernel(x_ref, w_ref, o_ref, xb_scr):
    @pl.when(pl.program_id(1) == 0)
    def _():
        xb_scr[...] = x_ref[...].astype(BF16)

    o_ref[...] = jnp.dot(xb_scr[...], w_ref[...], preferred_element_type=F32)


def _project(x2, w_seg, tm, tn):
    T, D = x2.shape
    n_cols = w_seg.shape[1]
    assert n_cols % tn == 0 and tn % LANES == 0 and T % tm == 0
    return pl.pallas_call(
        _proj_kernel,
        name="proj",
        grid=(T // tm, n_cols // tn),
        in_specs=[pl.BlockSpec((tm, D), lambda i, j: (i, 0)),
                  pl.BlockSpec((D, tn), lambda i, j: (0, j))],
        out_specs=pl.BlockSpec((tm, tn), lambda i, j: (i, j)),
        out_shape=jax.ShapeDtypeStruct((T, n_cols), F32),
        scratch_shapes=[pltpu.VMEM((tm, D), BF16)],
        compiler_params=_cparams(("parallel", "arbitrary")),
    )(x2, w_seg)


def _qkv_rope_kernel(x_ref, w_ref, cos_ref, sin_ref, o_ref, xb_scr):
    j = pl.program_id(1)

    @pl.when(j == 0)
    def _():
        xb_scr[...] = x_ref[...].astype(BF16)

    acc = jnp.dot(xb_scr[...], w_ref[...], preferred_element_type=F32)

    @pl.when(j < 2 * A_GROUPS)
    def _():
        lane = lax.broadcasted_iota(jnp.int32, acc.shape, 1)
        first = (lane % A_DH) < (A_DH // 2)
        rot = jnp.where(first, pltpu.roll(acc, A_W - A_DH // 2, 1), pltpu.roll(acc, A_DH // 2, 1))
        o_ref[0] = acc * cos_ref[...] + rot * sin_ref[...]

    @pl.when(j >= 2 * A_GROUPS)
    def _():
        o_ref[0] = acc


def _project_qkv(x2, w_bf16, cos_t, sin_t, tm, period_blocks):
    T, D = x2.shape
    return pl.pallas_call(
        _qkv_rope_kernel,
        name="qkv_rope",
        grid=(T // tm, 3 * A_GROUPS),
        in_specs=[pl.BlockSpec((tm, D), lambda i, j: (i, 0)),
                  pl.BlockSpec((D, A_W), lambda i, j: (0, j)),
                  pl.BlockSpec((tm, A_W), lambda i, j: (i % period_blocks, 0)),
                  pl.BlockSpec((tm, A_W), lambda i, j: (i % period_blocks, 0))],
        out_specs=pl.BlockSpec((1, tm, A_W), lambda i, j: (j, i, 0)),
        out_shape=jax.ShapeDtypeStruct((3 * A_GROUPS, T, A_W), F32),
        scratch_shapes=[pltpu.VMEM((tm, D), BF16)],
        compiler_params=_cparams(("parallel", "arbitrary")),
    )(x2, w_bf16, cos_t, sin_t)


def _rope_tables(pos, rows):
    half = A_DH // 2
    inv = ROPE_THETA ** (-jnp.arange(half, dtype=F32) / half)
    ang = pos.astype(F32)[:, None] * inv[None, :]
    cos = jnp.cos(ang)
    sin = jnp.sin(ang)
    cos_h = jnp.concatenate([cos, cos], -1)
    sin_h = jnp.concatenate([-sin, sin], -1)
    cos_f = jnp.tile(cos_h, (rows // pos.shape[0], A_HEADS))
    sin_f = jnp.tile(sin_h, (rows // pos.shape[0], A_HEADS))
    return cos_f, sin_f


def _attn_prompt_kernel(q_ref, kc_ref, kp_ref, vc_ref, vp_ref, o_ref, lse_ref):
    jb = pl.program_id(2)
    two = 2 * A_BLK
    r2 = lax.broadcasted_iota(jnp.int32, (two, LANES), 0)
    l2 = lax.broadcasted_iota(jnp.int32, (two, LANES), 1)
    qi = r2 % A_BLK
    own_half = (r2 < A_BLK) == (l2 < A_DH)
    mask_c = l2 <= qi
    mask_p = (l2 >= qi) & (jb > 0)
    l1 = lax.broadcasted_iota(jnp.int32, (A_BLK, LANES), 1)
    first = l1 < A_DH
    for p in range(A_HEADS // 2):
        sl = slice(p * LANES, (p + 1) * LANES)
        q = q_ref[0, 0, :, sl] * (A_DH ** -0.5)
        q2 = jnp.where(own_half, jnp.concatenate([q, q], axis=0), 0.0)
        s_c = jnp.where(mask_c, _bdot_nt(q2, kc_ref[0, 0, :, sl]), NEG_BIG)
        s_p = jnp.where(mask_p, _bdot_nt(q2, kp_ref[0, 0, :, sl]), NEG_BIG)
        m = jnp.maximum(jnp.max(s_c, -1, keepdims=True), jnp.max(s_p, -1, keepdims=True))
        p_c = jnp.exp(s_c - m)
        p_p = jnp.exp(s_p - m)
        l = jnp.sum(p_c, -1, keepdims=True) + jnp.sum(p_p, -1, keepdims=True)
        o2 = (_bdot(p_c, vc_ref[0, 0, :, sl]) + _bdot(p_p, vp_ref[0, 0, :, sl])) / l
        lse2 = jnp.broadcast_to(m + jnp.log(l), (two, LANES))
        o_ref[0, 0, :, sl] = jnp.where(first, o2[:A_BLK], o2[A_BLK:])
        lse_ref[0, 0, :, sl] = jnp.where(first, lse2[:A_BLK], lse2[A_BLK:])


def _attn_prompt(q, k, v):
    n, d, R, _ = q.shape
    nb = R // A_BLK
    cur = pl.BlockSpec((1, 1, A_BLK, A_W), lambda b, r, j: (b, r, j, 0))
    prv = pl.BlockSpec((1, 1, A_BLK, A_W), lambda b, r, j: (b, r, jnp.maximum(j - 1, 0), 0))
    shp = jax.ShapeDtypeStruct((n, d, R, A_W), F32)
    return pl.pallas_call(
        _attn_prompt_kernel,
        name="attn_prompt",
        grid=(n, d, nb),
        in_specs=[cur, cur, prv, cur, prv],
        out_specs=[cur, cur],
        out_shape=[shp, shp],
        compiler_params=_cparams(("parallel", "parallel", "arbitrary")),
    )(q, k, k, v, v)


def _attn_sample_kernel(qkv_ref, c0_ref, c1_ref, c2_ref, o_ref, *, n_new):
    caches = (c0_ref, c1_ref, c2_ref)
    T = n_new
    masks = []
    for g in range(A_GROUPS):
        d = A_DILATIONS[g]
        W = A_WINDOWS[g]
        back_c = lax.broadcasted_iota(jnp.int32, (T, W), 0) - lax.broadcasted_iota(jnp.int32, (T, W), 1)
        back_n = lax.broadcasted_iota(jnp.int32, (T, T), 0) - lax.broadcasted_iota(jnp.int32, (T, T), 1)
        masks.append(((back_c <= 0) & ((back_c & (d - 1)) == 0), (back_n >= 0) & ((back_n & (d - 1)) == 0)))
    for h in range(A_HEADS):
        sl = slice(h * A_DH, (h + 1) * A_DH)
        outs, lses = [], []
        for g in range(A_GROUPS):
            c_ref = caches[g]
            mask_c, mask_n = masks[g]
            q = qkv_ref[g, 0, :, sl] * (A_DH ** -0.5)
            s_c = jnp.where(mask_c, _bdot(q, c_ref[0, h]), NEG_BIG)
            s_n = jnp.where(mask_n, _bdot_nt(q, qkv_ref[A_GROUPS + g, 0, :, sl]), NEG_BIG)
            m = jnp.maximum(jnp.max(s_c, -1, keepdims=True), jnp.max(s_n, -1, keepdims=True))
            p_c = jnp.exp(s_c - m)
            p_n = jnp.exp(s_n - m)
            l = jnp.sum(p_c, -1, keepdims=True) + jnp.sum(p_n, -1, keepdims=True)
            o = _bdot_nt(p_c, c_ref[1, h]) + _bdot(p_n, qkv_ref[2 * A_GROUPS + g, 0, :, sl])
            outs.append(o / l)
            lses.append(m + jnp.log(l))
        mx = jnp.maximum(jnp.maximum(lses[0], lses[1]), lses[2])
        ws = [jnp.exp(x - mx) for x in lses]
        den = ws[0] + ws[1] + ws[2]
        o_ref[0, :, sl] = (ws[0] * outs[0] + ws[1] * outs[1] + ws[2] * outs[2]) / den


def _attn_sample(qkv4, caches_t, layer):
    _, N, T, _ = qkv4.shape
    in_specs = [pl.BlockSpec((3 * A_GROUPS, 1, T, A_W), lambda n: (0, n, 0, 0))]
    for g in range(A_GROUPS):
        assert caches_t[g].shape[2:] == (2, A_HEADS, A_DH, A_WINDOWS[g])
        in_specs.append(pl.BlockSpec((None, None, 2, A_HEADS, A_DH, A_WINDOWS[g]),
                                     lambda n: (layer, n, 0, 0, 0, 0)))
    return pl.pallas_call(
        functools.partial(_attn_sample_kernel, n_new=T),
        name="attn_sample",
        grid=(N,),
        in_specs=in_specs,
        out_specs=pl.BlockSpec((1, T, A_W), lambda n: (n, 0, 0)),
        out_shape=jax.ShapeDtypeStruct((N, T, A_W), F32),
        compiler_params=_cparams(("parallel",)),
    )(qkv4, *caches_t)


_DIMS = {"nn": (((1,), (0,)), ((), ())), "nt": (((1,), (1,)), ((), ())), "tn": (((0,), (0,)), ((), ()))}


def _mm(a, b, mode="nn"):
    dn = _DIMS[mode]
    a_hi = a.astype(BF16)
    a_lo = (a - a_hi.astype(F32)).astype(BF16)
    b_hi = b.astype(BF16)
    b_lo = (b - b_hi.astype(F32)).astype(BF16)
    out = lax.dot_general(a_lo, b_hi, dn, preferred_element_type=F32)
    out = out + lax.dot_general(a_hi, b_lo, dn, preferred_element_type=F32)
    return out + lax.dot_general(a_hi, b_hi, dn, preferred_element_type=F32)


def _tri_incl(c):
    t = lax.broadcasted_iota(jnp.int32, (c, c), 0)
    s = lax.broadcasted_iota(jnp.int32, (c, c), 1)
    return t, s


SOLVE_BLOCK = 16


_BDIMS = {"nn": (((2,), (1,)), ((0,), (0,))), "nt": (((2,), (2,)), ((0,), (0,))), "tn": (((1,), (1,)), ((0,), (0,)))}


def _bmm(a, b, mode="nn", split=True):
    dn = _BDIMS[mode]
    a_hi = a.astype(BF16)
    if not split:
        return lax.dot_general(a_hi, b.astype(BF16), dn, preferred_element_type=F32)
    a_lo = (a - a_hi.astype(F32)).astype(BF16)
    b_hi = b.astype(BF16)
    b_lo = (b - b_hi.astype(F32)).astype(BF16)
    out = lax.dot_general(a_lo, b_hi, dn, preferred_element_type=F32)
    out = out + lax.dot_general(a_hi, b_lo, dn, preferred_element_type=F32)
    return out + lax.dot_general(a_hi, b_hi, dn, preferred_element_type=F32)


def _solve_unit_lower(lm, z, C):
    blk = min(SOLVE_BLOCK, C)
    done = []
    for j in range(C // blk):
        r0 = j * blk
        zj = z[:, r0:r0 + blk]
        if j > 0:
            zj = zj - _bmm(lm[:, r0:r0 + blk, :r0], jnp.concatenate(done, 1), split=False)
        ljj = lm[:, r0:r0 + blk, r0:r0 + blk]
        for s in range(blk - 1):
            zj = zj - ljj[:, :, s:s + 1] * zj[:, s:s + 1, :]
        done.append(zj)
    return jnp.concatenate(done, 1)


def _rwkv_kernel(bc_ref, bprev_ref, s0_ref, mu_ref, w0_ref, a0_ref, kkp_ref, ka_ref, w2_ref, a2_ref, g2_ref,
                 rk_ref, lng_ref, lnb_ref, o_ref, sout_ref,
                 prev_scr, s_scr, h_gam, h_rt, h_kt, h_v, h_kk, h_ebp, h_aenb, h_rk, h_g, h_y, *, C):
    c = pl.program_id(1)
    H, E = B_HEADS, B_DH

    @pl.when(c == 0)
    def _():
        prev_scr[...] = bprev_ref[0]
        s_scr[...] = s0_ref[0]

    bc = bc_ref[0]
    row = lax.broadcasted_iota(jnp.int32, bc.shape, 0)
    prev = jnp.where(row == 0, prev_scr[...], pltpu.roll(bc, 1, 0))
    prev_scr[...] = bc[C - 1:C, :]
    mixed = bc + mu_ref[...] * (prev - bc)
    r = mixed[:, 0:B_W]
    k = mixed[:, B_W:2 * B_W]
    v = mixed[:, 2 * B_W:3 * B_W]
    o1 = 3 * B_W
    wl = mixed[:, o1:o1 + B_W_LORA]
    al = mixed[:, o1 + B_W_LORA:o1 + B_W_LORA + B_A_LORA]
    gl = mixed[:, o1 + B_W_LORA + B_A_LORA:]
    w_raw = -_softplus(-(w0_ref[...] + _mm(jnp.tanh(wl), w2_ref[...]))) - 0.5
    lw = -jnp.exp(w_raw)
    a = _sigmoid(a0_ref[...] + _mm(al, a2_ref[...]))
    gate = _mm(_sigmoid(gl), g2_ref[...])
    kkr = k * kkp_ref[...]
    kb = k * (1.0 + (a - 1.0) * ka_ref[...])
    t_i, s_i = _tri_incl(C)
    b = _dot_exact_lhs(s_i <= t_i, lw)
    eb = jnp.exp(b)
    enb = jnp.exp(-b)
    parts = (r * eb, kb * enb, v, kkr, jnp.exp(b - lw), a * enb, r * kb * rk_ref[...], gate)
    for h in range(H):
        sl = slice(h * E, (h + 1) * E)
        for dst, src in zip((h_rt, h_kt, h_v, h_kk, h_ebp, h_aenb, h_rk, h_g), parts):
            dst[h] = src[:, sl]
        h_gam[h] = eb[C - 1:C, sl]

    strict = (s_i < t_i)[None]
    incl = (s_i <= t_i)[None]
    eye = (lax.broadcasted_iota(jnp.int32, (E, E), 0) == lax.broadcasted_iota(jnp.int32, (E, E), 1))[None]
    rt, kt, v3, gam = h_rt[...], h_kt[...], h_v[...], h_gam[...]
    kk = h_kk[...]
    kk = kk * lax.rsqrt(jnp.maximum(jnp.sum(kk * kk, -1, keepdims=True), 1e-24))
    at = kk * h_ebp[...]
    bt = kk * h_aenb[...]
    x = _bmm(jnp.concatenate([at, rt], 1), jnp.concatenate([bt, kt], 1), "nt", split=False)
    lm = jnp.where(strict, x[:, :C, :C], 0.0)
    mk = jnp.where(strict, x[:, :C, C:], 0.0)
    arb = jnp.where(incl, x[:, C:, :C], 0.0)
    ark = jnp.where(incl, x[:, C:, C:], 0.0)
    z = _solve_unit_lower(lm, jnp.concatenate([at, _bmm(mk, v3, split=False)], 2), C)
    qy = jnp.concatenate([rt, _bmm(ark, v3, split=False)], 2) - _bmm(arb, z, split=False)
    top = jnp.where(eye, gam, 0.0)
    gd = (jnp.concatenate([top, _bmm(v3, kt * gam, "tn", split=False)], 1)
          - _bmm(z, bt * gam, "tn", split=False))
    s_all = s_scr[...]
    y = _bmm(qy[:, :, :E], s_all, "nt", split=False) + qy[:, :, E:]
    s_scr[...] = _bmm(s_all, gd[:, :E]) + gd[:, E:]
    mu_y = jnp.mean(y, -1, keepdims=True)
    var = jnp.mean(jnp.square(y - mu_y), -1, keepdims=True)
    yn = (y - mu_y) * lax.rsqrt(var + B_GN_EPS) * lng_ref[...] + lnb_ref[...]
    bonus = jnp.sum(h_rk[...], -1, keepdims=True) * v3
    h_y[...] = (yn + bonus) * h_g[...]
    for h in range(H):
        o_ref[0, :, h * E:(h + 1) * E] = h_y[h]

    @pl.when(c == pl.num_programs(1) - 1)
    def _():
        sout_ref[0] = s_scr[...]


def _rwkv(bcols, b_prev, s0, p, C):
    N, L, _ = bcols.shape
    H, E = B_HEADS, B_DH
    row = lambda a: a.reshape(1, -1)
    per_head = lambda a: a.reshape(H, 1, E)
    full = lambda shape: pl.BlockSpec(shape, lambda n, c: (0,) * len(shape))
    hs = lambda: pltpu.VMEM((H, C, E), F32)
    return pl.pallas_call(
        functools.partial(_rwkv_kernel, C=C),
        name="rwkv",
        grid=(N, L // C),
        in_specs=[pl.BlockSpec((1, C, B_COLS), lambda n, c: (n, c, 0)),
                  pl.BlockSpec((1, 1, B_COLS), lambda n, c: (n, 0, 0)),
                  pl.BlockSpec((1, H, E, E), lambda n, c: (n, 0, 0, 0)),
                  full((1, B_COLS)), full((1, B_W)), full((1, B_W)), full((1, B_W)), full((1, B_W)),
                  full((B_W_LORA, B_W)), full((B_A_LORA, B_W)), full((B_G_LORA, B_W)),
                  full((1, B_W)), full((H, 1, E)), full((H, 1, E))],
        out_specs=[pl.BlockSpec((1, C, B_W), lambda n, c: (n, c, 0)),
                   pl.BlockSpec((1, H, E, E), lambda n, c: (n, 0, 0, 0))],
        out_shape=[jax.ShapeDtypeStruct((N, L, B_W), F32), jax.ShapeDtypeStruct((N, H, E, E), F32)],
        scratch_shapes=[pltpu.VMEM((1, B_COLS), F32), pltpu.VMEM((H, E, E), F32), pltpu.VMEM((H, 1, E), F32)]
                       + [hs() for _ in range(9)],
        compiler_params=_cparams(("parallel", "arbitrary")),
    )(bcols, b_prev.reshape(N, 1, B_COLS), s0, row(p['b_mu']), row(p['b_w0']), row(p['b_a0']), row(p['b_kk']),
      row(p['b_ka']), p['b_w2'], p['b_a2'], p['b_g2'], row(p['b_rk']), per_head(p['b_lnx_g']),
      per_head(p['b_lnx_b']))


def _hgrn_levels(C):
    return int(math.log2(C))


def _hgrn_kernel(ch_ref, h0_ref, lb_ref, cng_ref, o_ref, hout_ref,
                 h_scr, hq, hk, hi, hb, hgt, hbp, ho, *, C):
    c = pl.program_id(1)
    H, E = C_HEADS, C_DK
    n_lv = _hgrn_levels(C)

    @pl.when(c == 0)
    def _():
        h_scr[...] = h0_ref[0]

    ch = ch_ref[0]
    cq = ch[:, 0:C_W]
    z = ch[:, C_W:2 * C_W]
    ci = ch[:, 2 * C_W:3 * C_W]
    cg = ch[:, 3 * C_W:4 * C_W]
    lb = lb_ref[...]
    loglb = jnp.log(jnp.maximum(lb, C_LB_FLOOR))
    log_f = -_softplus(-z) + _softplus(loglb - z)
    kc = (1.0 - lb) * _sigmoid(-z)
    qc = cq * _sigmoid(cq)
    gt = cg * _sigmoid(cg)
    t_i, s_i = _tri_incl(C)
    b = _dot_exact_lhs(s_i <= t_i, log_f)
    bps = []
    for lv in range(n_lv):
        m = 1 << lv
        sel = s_i == (t_i - (t_i % (2 * m)) + m - 1)
        bps.append(_dot_exact_lhs(sel, b))
    for h in range(H):
        sl = slice(h * E, (h + 1) * E)
        hq[h] = qc[:, sl]
        hk[h] = kc[:, sl]
        hi[h] = ci[:, sl]
        hb[h] = b[:, sl]
        hgt[h] = gt[:, sl]
        for lv in range(n_lv):
            hbp[lv * H + h] = bps[lv][:, sl]

    row_e = lax.broadcasted_iota(jnp.int32, (1, C, E), 1)
    eye = (lax.broadcasted_iota(jnp.int32, (E, E), 0) == lax.broadcasted_iota(jnp.int32, (E, E), 1))[None]
    q3, k3, i3, b3 = hq[...], hk[...], hi[...], hb[...]
    att = jnp.where((t_i == s_i)[None], _bmm(q3, k3, "nt", split=False), 0.0)
    for lv in range(n_lv):
        m = 1 << lv
        bp = hbp[lv * H:(lv + 1) * H]
        upper = (row_e % (2 * m)) >= m
        qm = jnp.where(upper, q3 * jnp.exp(jnp.minimum(b3 - bp, 0.0)), 0.0)
        km = jnp.where(upper, 0.0, k3 * jnp.exp(jnp.minimum(bp - b3, 0.0)))
        same = ((t_i // (2 * m)) == (s_i // (2 * m)))[None]
        att = att + jnp.where(same, _bmm(qm, km, "nt", split=False), 0.0)
    h_m = h_scr[...]
    o = _bmm(q3 * jnp.exp(b3), h_m, split=False) + _bmm(att, i3, split=False)
    b_last = b3[:, C - 1:C, :]
    dg = jnp.where(eye, jnp.exp(b_last), 0.0)
    kdec = k3 * jnp.exp(b_last - b3)
    h_scr[...] = _bmm(jnp.concatenate([dg, kdec], 1), jnp.concatenate([h_m, i3], 1), "tn")
    o = o * lax.rsqrt(jnp.mean(jnp.square(o), -1, keepdims=True) + RMS_EPS) * cng_ref[...]
    ho[...] = o * hgt[...]
    for h in range(H):
        o_ref[0, :, h * E:(h + 1) * E] = ho[h]

    @pl.when(c == pl.num_programs(1) - 1)
    def _():
        hout_ref[0] = h_scr[...]


def _hgrn(ch, h0, lb, c_norm_g, C):
    N, L, _ = ch.shape
    H, E = C_HEADS, C_DK
    n_lv = _hgrn_levels(C)
    full = lambda shape: pl.BlockSpec(shape, lambda n, c: (0,) * len(shape))
    hs = lambda k=1: pltpu.VMEM((k * H, C, E), F32)
    return pl.pallas_call(
        functools.partial(_hgrn_kernel, C=C),
        name="hgrn",
        grid=(N, L // C),
        in_specs=[pl.BlockSpec((1, C, 4 * C_W), lambda n, c: (n, c, 0)),
                  pl.BlockSpec((1, H, E, E), lambda n, c: (n, 0, 0, 0)),
                  full((1, C_W)), full((1, E))],
        out_specs=[pl.BlockSpec((1, C, C_W), lambda n, c: (n, c, 0)),
                   pl.BlockSpec((1, H, E, E), lambda n, c: (n, 0, 0, 0))],
        out_shape=[jax.ShapeDtypeStruct((N, L, C_W), F32), jax.ShapeDtypeStruct((N, H, E, E), F32)],
        scratch_shapes=[pltpu.VMEM((H, E, E), F32), hs(), hs(), hs(), hs(), hs(), hs(n_lv), hs()],
        compiler_params=_cparams(("parallel", "arbitrary")),
    )(ch, h0, lb.reshape(1, C_W), c_norm_g.reshape(1, E))


def _layer_norm(v, g, b):
    mu = jnp.mean(v, -1, keepdims=True)
    var = jnp.mean(jnp.square(v - mu), -1, keepdims=True)
    return (v - mu) * lax.rsqrt(var + LN_EPS) * g + b


def _merge_kernel(*refs, n_attn, tm):
    attn = refs[:n_attn]
    (ob_ref, oc_ref, gates_ref, x_ref, wa_ref, wb_ref, wc_ref, wo_ref, lng_ref, lnb_ref, wr_ref, br_ref,
     x1_ref, idx_ref, gate_ref, rank_ref, cnt_ref, run_scr) = refs[n_attn:]
    i = pl.program_id(0)

    @pl.when(i == 0)
    def _():
        run_scr[...] = jnp.zeros_like(run_scr)

    if n_attn == 1:
        o_a = attn[0][...]
    else:
        o0, l0, o1, l1, o2, l2 = (r[...] for r in attn)
        mx = jnp.maximum(jnp.maximum(l0, l1), l2)
        w0, w1, w2 = jnp.exp(l0 - mx), jnp.exp(l1 - mx), jnp.exp(l2 - mx)
        o_a = (w0 * o0 + w1 * o1 + w2 * o2) / (w0 + w1 + w2)
    gates = gates_ref[...]
    merged = (_sigmoid(gates[:, 0:D_MODEL]) * _bdot(o_a, wa_ref[...])
              + _sigmoid(gates[:, D_MODEL:2 * D_MODEL]) * _bdot(ob_ref[...], wb_ref[...])
              + _sigmoid(gates[:, 2 * D_MODEL:]) * _bdot(oc_ref[...], wc_ref[...]))
    mix = _bdot(merged, wo_ref[...])
    x1 = _layer_norm(DN_ALPHA * x_ref[...] + mix, lng_ref[...], lnb_ref[...])
    x1_ref[...] = x1

    logits = _bdot(x1, wr_ref[...]) + br_ref[...]
    lane = lax.broadcasted_iota(jnp.int32, (tm, LANES), 1)
    cur = jnp.where(lane < N_EXPERTS, logits, -jnp.inf)
    vals, idxs, sels = [], [], []
    for _ in range(TOP_K):
        mx = jnp.max(cur, -1, keepdims=True)
        idx = jnp.min(jnp.where(cur == mx, lane, LANES), -1, keepdims=True)
        sel = lane == idx
        vals.append(mx)
        idxs.append(idx)
        sels.append(sel)
        cur = jnp.where(sel, -jnp.inf, cur)
    es = [jnp.exp(v - vals[0]) for v in vals]
    den = es[0] + es[1] + es[2] + es[3]
    mask = jnp.zeros((tm, LANES), F32)
    for sel in sels:
        mask = jnp.where(sel, 1.0, mask)
    r_i = lax.broadcasted_iota(jnp.int32, (tm, tm), 0)
    c_i = lax.broadcasted_iota(jnp.int32, (tm, tm), 1)
    excl = jnp.dot((c_i < r_i).astype(BF16), mask.astype(BF16), preferred_element_type=F32) + run_scr[...]
    idx_o = jnp.zeros((tm, LANES), jnp.int32)
    gate_o = jnp.zeros((tm, LANES), F32)
    rank_o = jnp.zeros((tm, LANES), F32)
    for j in range(TOP_K):
        rank_j = jnp.sum(jnp.where(sels[j], excl, 0.0), -1, keepdims=True)
        idx_o = jnp.where(lane == j, idxs[j], idx_o)
        gate_o = jnp.where(lane == j, es[j] / den, gate_o)
        rank_o = jnp.where(lane == j, rank_j, rank_o)
    idx_ref[...] = idx_o
    gate_ref[...] = gate_o
    rank_ref[...] = rank_o.astype(jnp.int32)
    run_scr[...] = run_scr[...] + jnp.sum(mask, axis=0, keepdims=True)
    cnt_ref[...] = jnp.broadcast_to(run_scr[...], cnt_ref.shape)


def _merge(attn, o_b, o_c, gates, x2, wts, tm):
    T = x2.shape[0]
    n_attn = len(attn)
    rowspec = lambda w: pl.BlockSpec((tm, w), lambda i: (i, 0))
    full = lambda a: pl.BlockSpec(a.shape, lambda i: (0,) * a.ndim)
    wlist = [wts['w_br_a'], wts['w_br_b'], wts['w_br_c'], wts['w_out'], wts['ln_g0'], wts['ln_b0'],
             wts['w_router'], wts['b_router']]
    return pl.pallas_call(
        functools.partial(_merge_kernel, n_attn=n_attn, tm=tm),
        name="merge",
        grid=(T // tm,),
        in_specs=[rowspec(A_W)] * n_attn + [rowspec(B_W), rowspec(C_W), rowspec(3 * D_MODEL), rowspec(D_MODEL)]
                 + [full(w) for w in wlist],
        out_specs=[rowspec(D_MODEL), rowspec(LANES), rowspec(LANES), rowspec(LANES),
                   pl.BlockSpec((SUBLANES, LANES), lambda i: (0, 0))],
        out_shape=[jax.ShapeDtypeStruct((T, D_MODEL), F32), jax.ShapeDtypeStruct((T, LANES), jnp.int32),
                   jax.ShapeDtypeStruct((T, LANES), F32), jax.ShapeDtypeStruct((T, LANES), jnp.int32),
                   jax.ShapeDtypeStruct((SUBLANES, LANES), F32)],
        scratch_shapes=[pltpu.VMEM((1, LANES), F32)],
        compiler_params=_cparams(("arbitrary",)),
    )(*attn, o_b, o_c, gates, x2, *wlist)


def _dispatch_kernel(dest_ref, x_ref, xs_in_ref, xs_ref, sem, *, td):
    del xs_in_ref

    n_rows = td * TOP_K

    def issue(t, carry):
        for j in range(TOP_K):
            pltpu.make_async_copy(x_ref.at[pl.ds(t, 1)], xs_ref.at[pl.ds(dest_ref[0, 0, t * TOP_K + j], 1)],
                                  sem).start()
        return carry

    lax.fori_loop(0, td, issue, 0, unroll=2)
    pltpu.make_async_copy(xs_ref.at[pl.ds(0, n_rows)], xs_ref.at[pl.ds(0, n_rows)], sem).wait()


def _dispatch(x1, dest, n_slots, td):
    T = x1.shape[0]
    zeros = jnp.zeros((n_slots, D_MODEL), F32)
    return pl.pallas_call(
        functools.partial(_dispatch_kernel, td=td),
        name="dispatch",
        grid=(T // td,),
        in_specs=[pl.BlockSpec((1, 1, td * TOP_K), lambda i: (i, 0, 0), memory_space=pltpu.SMEM),
                  pl.BlockSpec((td, D_MODEL), lambda i: (i, 0)),
                  pl.BlockSpec(memory_space=pl.ANY)],
        out_specs=pl.BlockSpec(memory_space=pl.ANY),
        out_shape=jax.ShapeDtypeStruct((n_slots, D_MODEL), F32),
        scratch_shapes=[pltpu.SemaphoreType.DMA(())],
        input_output_aliases={2: 0},
        compiler_params=_cparams(("arbitrary",)),
    )(dest.reshape(T // td, 1, td * TOP_K), x1, zeros)


def _expert_kernel(be_ref, nu_ref, xs_ref, wu_ref, bu_ref, wd_ref, bd_ref, o_ref):
    b = pl.program_id(0)

    @pl.when(b < nu_ref[0])
    def _():
        h = jnp.dot(xs_ref[...].astype(BF16), wu_ref[0], preferred_element_type=F32) + bu_ref[0]
        h_glu = jnp.minimum(h[:, :D_FF], SWIGLU_LIMIT)
        h_lin = jnp.clip(h[:, D_FF:], -SWIGLU_LIMIT, SWIGLU_LIMIT)
        act = h_glu * _sigmoid(SWIGLU_ALPHA * h_glu) * (h_lin + 1.0)
        o_ref[...] = jnp.dot(act.astype(BF16), wd_ref[0], preferred_element_type=F32) + bd_ref[0]

    @pl.when(b >= nu_ref[0])
    def _():
        o_ref[...] = jnp.zeros_like(o_ref)


def _experts(xs, block_expert, n_used, w_up, b_up, w_down, b_down):
    n_slots = xs.shape[0]
    n_blocks = n_slots // MOE_BM
    grid_spec = pltpu.PrefetchScalarGridSpec(
        num_scalar_prefetch=2,
        grid=(n_blocks,),
        in_specs=[pl.BlockSpec((MOE_BM, D_MODEL), lambda b, be, nu: (b, 0)),
                  pl.BlockSpec((1, D_MODEL, 2 * D_FF), lambda b, be, nu: (be[b], 0, 0)),
                  pl.BlockSpec((1, 1, 2 * D_FF), lambda b, be, nu: (be[b], 0, 0)),
                  pl.BlockSpec((1, D_FF, D_MODEL), lambda b, be, nu: (be[b], 0, 0)),
                  pl.BlockSpec((1, 1, D_MODEL), lambda b, be, nu: (be[b], 0, 0))],
        out_specs=pl.BlockSpec((MOE_BM, D_MODEL), lambda b, be, nu: (b, 0)),
    )
    return pl.pallas_call(
        _expert_kernel,
        name="experts",
        grid_spec=grid_spec,
        out_shape=jax.ShapeDtypeStruct((n_slots, D_MODEL), F32),
        compiler_params=_cparams(("arbitrary",)),
    )(block_expert, n_used, xs, w_up, b_up.reshape(N_EXPERTS, 1, 2 * D_FF), w_down,
      b_down.reshape(N_EXPERTS, 1, D_MODEL))


def _combine_kernel(dest_ref, gate_ref, x1_ref, eo_ref, lng_ref, lnb_ref, o_ref, buf, sem, *, tc):
    n_rows = tc * TOP_K

    def issue(t, carry):
        for j in range(TOP_K):
            pltpu.make_async_copy(eo_ref.at[pl.ds(dest_ref[0, 0, t * TOP_K + j], 1)],
                                  buf.at[pl.ds(j * tc + t, 1)], sem).start()
        return carry

    lax.fori_loop(0, tc, issue, 0, unroll=2)
    pltpu.make_async_copy(eo_ref.at[pl.ds(0, n_rows)], buf, sem).wait()
    gate = gate_ref[...]
    y = gate[:, 0:1] * buf[0:tc]
    for j in range(1, TOP_K):
        y = y + gate[:, j:j + 1] * buf[j * tc:(j + 1) * tc]
    o_ref[...] = _layer_norm(DN_ALPHA * x1_ref[...] + y, lng_ref[...], lnb_ref[...])


def _combine(dest, gate, x1, eo, ln_g, ln_b, tc):
    T = x1.shape[0]
    full = lambda a: pl.BlockSpec(a.shape, lambda i: (0,) * a.ndim)
    return pl.pallas_call(
        functools.partial(_combine_kernel, tc=tc),
        name="combine",
        grid=(T // tc,),
        in_specs=[pl.BlockSpec((1, 1, tc * TOP_K), lambda i: (i, 0, 0), memory_space=pltpu.SMEM),
                  pl.BlockSpec((tc, LANES), lambda i: (i, 0)),
                  pl.BlockSpec((tc, D_MODEL), lambda i: (i, 0)),
                  pl.BlockSpec(memory_space=pl.ANY), full(ln_g), full(ln_b)],
        out_specs=pl.BlockSpec((tc, D_MODEL), lambda i: (i, 0)),
        out_shape=jax.ShapeDtypeStruct((T, D_MODEL), F32),
        scratch_shapes=[pltpu.VMEM((TOP_K * tc, D_MODEL), F32), pltpu.SemaphoreType.DMA(())],
        compiler_params=_cparams(("arbitrary",)),
    )(dest.reshape(T // tc, 1, tc * TOP_K), gate, x1, eo, ln_g, ln_b)


def _moe(x1, top_idx, top_gate, top_rank, counts, wts):
    T = x1.shape[0]
    cnt = counts[0, :N_EXPERTS].astype(jnp.int32)
    padded = (cnt + MOE_BM - 1) // MOE_BM * MOE_BM
    pend = jnp.cumsum(padded)
    pstart = pend - padded
    dest = (pstart[top_idx[:, :TOP_K]] + top_rank[:, :TOP_K]).astype(jnp.int32)
    n_blocks = T * TOP_K // MOE_BM + N_EXPERTS
    block_start = jnp.arange(n_blocks, dtype=jnp.int32) * MOE_BM
    block_expert = jnp.minimum(jnp.sum(pend[None, :] <= block_start[:, None], axis=1), N_EXPERTS - 1).astype(jnp.int32)
    n_used = (pend[-1:] // MOE_BM).astype(jnp.int32)
    xs = _dispatch(x1, dest, n_blocks * MOE_BM, min(256, T))
    eo = _experts(xs, block_expert, n_used, wts['w_up'], wts['b_up'], wts['w_down'], wts['b_down'])
    return _combine(dest, top_gate, x1, eo, wts['ln_g1'], wts['ln_b1'], min(128, T))


def _hybrid_layer(x, pos, caches, layer, b_shift, b_wkv, c_st, lb, wts):
    N, L, D = x.shape
    T = N * L
    x2 = x.reshape(T, D)
    tm = min(1024, T)
    rows = max(L, tm)
    cos_t, sin_t = _rope_tables(pos, rows)
    qkv = _project_qkv(x2, wts['w_qkv'], cos_t, sin_t, tm, rows // tm)
    bcols = _project(x2, wts['w_b'], tm, B_COLS // 2).reshape(N, L, B_COLS)
    ch = _project(x2, wts['w_c'], tm, 2 * C_W).reshape(N, L, 4 * C_W)
    gates = _project(x2, wts['w_g'], tm, 3 * D_MODEL // 2)

    a_new = []
    if caches is None:
        attn = []
        for g in range(A_GROUPS):
            d = A_DILATIONS[g]
            to_res = lambda t: t.reshape(N, L // d, d, A_W).transpose(0, 2, 1, 3)
            o, lse = _attn_prompt(to_res(qkv[g]), to_res(qkv[A_GROUPS + g]), to_res(qkv[2 * A_GROUPS + g]))
            from_res = lambda t: t.transpose(0, 2, 1, 3).reshape(T, A_W)
            attn += [from_res(o), from_res(lse)]
            keep = min(A_WINDOWS[g], L)
            kg = qkv[A_GROUPS + g].reshape(N, L, A_HEADS, A_DH)[:, L - keep:]
            vg = qkv[2 * A_GROUPS + g].reshape(N, L, A_HEADS, A_DH)[:, L - keep:]
            a_new.append(jnp.stack([kg, vg], axis=2))
    else:
        qkv5 = qkv.reshape(3 * A_GROUPS, N, L, A_HEADS, A_DH)
        attn = [_attn_sample(qkv.reshape(3 * A_GROUPS, N, L, A_W), caches, layer).reshape(T, A_W)]
        for g in range(A_GROUPS):
            a_new.append(jnp.stack([qkv5[A_GROUPS + g], qkv5[2 * A_GROUPS + g]], axis=2))

    o_b, b_s = _rwkv(bcols, b_shift, b_wkv, wts, min(RWKV_CHUNK, L))
    o_c, c_s = _hgrn(ch, c_st, lb, wts['c_norm_g'], min(HGRN_CHUNK, L))
    x1, top_idx, top_gate, top_rank, counts = _merge(
        attn, o_b.reshape(T, B_W), o_c.reshape(T, C_W), gates, x2, wts, min(512, T))
    x_out = _moe(x1, top_idx, top_gate, top_rank, counts, wts)
    return x_out.reshape(N, L, D), (a_new, bcols[:, -1], b_s, c_s)


def kernel(x_prompt, x_sample, cache_a0_kv, cache_a1_kv, cache_a2_kv, state_b_shift, state_b_wkv, state_c,
           w_in, b_mu, b_w0, b_w2, b_a0, b_a2, b_g2, b_kk, b_ka, b_rk, b_lnx_g, b_lnx_b, c_lb, c_norm_g,
           w_br_a, w_br_b, w_br_c, w_out, ln_g, ln_b, w_router, b_router, w_up, b_up, w_down, b_down):
    lb_all = jax.nn.softmax(c_lb.astype(F32), axis=0)
    lb_all = jnp.cumsum(lb_all, axis=0) - lb_all[0]
    pad_e = LANES - N_EXPERTS
    layers = []
    for l in range(DEPTH):
        layers.append({
            'w_qkv': w_in[l, :, :OFF_B].astype(BF16), 'w_b': w_in[l, :, OFF_B:OFF_C].astype(BF16),
            'w_c': w_in[l, :, OFF_C:OFF_G].astype(BF16), 'w_g': w_in[l, :, OFF_G:].astype(BF16), 'b_mu': b_mu[l], 'b_w0': b_w0[l], 'b_w2': b_w2[l], 'b_a0': b_a0[l],
            'b_a2': b_a2[l], 'b_g2': b_g2[l], 'b_kk': b_kk[l], 'b_ka': b_ka[l], 'b_rk': b_rk[l],
            'b_lnx_g': b_lnx_g[l], 'b_lnx_b': b_lnx_b[l], 'c_norm_g': c_norm_g[l],
            'w_br_a': w_br_a[l].astype(BF16), 'w_br_b': w_br_b[l].astype(BF16), 'w_br_c': w_br_c[l].astype(BF16),
            'w_out': w_out[l].astype(BF16),
            'ln_g0': ln_g[l, 0].reshape(1, D_MODEL), 'ln_b0': ln_b[l, 0].reshape(1, D_MODEL),
            'ln_g1': ln_g[l, 1].reshape(1, D_MODEL), 'ln_b1': ln_b[l, 1].reshape(1, D_MODEL),
            'w_router': jnp.pad(w_router[l].astype(F32), ((0, 0), (0, pad_e))),
            'b_router': jnp.pad(b_router[l].astype(F32), (0, pad_e)).reshape(1, LANES),
            'w_up': w_up[l].astype(BF16), 'b_up': b_up[l], 'w_down': w_down[l].astype(BF16), 'b_down': b_down[l]})

    def run_group(x, pos, caches, b_shift, b_wkv, c_st):
        new = ([], [], [], [], [], [])
        for l in range(DEPTH):
            x, (a_new, b_last, b_s, c_s) = _hybrid_layer(x, pos, caches, l, b_shift[l], b_wkv[l], c_st[l],
                                                         lb_all[l], layers[l])
            for g in range(A_GROUPS):
                new[g].append(a_new[g])
            new[3].append(b_last)
            new[4].append(b_s)
            new[5].append(c_s)
        return x, [jnp.stack(s) for s in new]

    nb_p, seq = x_prompt.shape[0], x_prompt.shape[1]
    y_prompt, (p_a0, p_a1, p_a2, p_bs, p_bw, p_c) = run_group(
        x_prompt, jnp.arange(seq, dtype=jnp.int32), None,
        jnp.zeros((DEPTH, nb_p, B_COLS), F32),
        jnp.zeros((DEPTH, nb_p, B_HEADS, B_DH, B_DH), F32),
        jnp.zeros((DEPTH, nb_p, C_HEADS, C_DK, C_DV), F32))
    pos_s = PAST_LEN + jnp.arange(x_sample.shape[1], dtype=jnp.int32)
    caches_t = tuple(jnp.transpose(c, (0, 1, 3, 4, 5, 2)) for c in (cache_a0_kv, cache_a1_kv, cache_a2_kv))
    y_sample, (s_a0, s_a1, s_a2, s_bs, s_bw, s_c) = run_group(
        x_sample, pos_s, caches_t, state_b_shift, state_b_wkv, state_c)
    return (y_prompt, y_sample, p_a0, p_a1, p_a2, p_bs, p_bw, p_c, s_a0, s_a1, s_a2, s_bs, s_bw, s_c)
```

```python
import functools
import math

import numpy as np
import jax
import jax.numpy as jnp
from jax import lax
from jax.experimental import pallas as pl
from jax.experimental.pallas import tpu as pltpu

F32 = jnp.float32
BF16 = jnp.bfloat16

D_MODEL = 1024
DEPTH = 2
PAST_LEN = 2048
A_WINDOWS = (128, 512, 2048)
A_DILATIONS = (1, 4, 16)
A_GROUPS = 3
A_HEADS = 8
A_DH = 64
A_W = A_HEADS * A_DH
A_BLK = 128
B_DH = 64
B_HEADS = 16
B_W = 1024
B_W_LORA = 64
B_A_LORA = 64
B_G_LORA = 128
B_COLS = 3 * B_W + B_W_LORA + B_A_LORA + B_G_LORA
B_GN_EPS = 64e-5
C_DK = 64
C_DV = 64
C_HEADS = 16
C_W = 1024
C_LB_FLOOR = 1e-30
N_EXPERTS = 32
TOP_K = 4
D_FF = 1024
SWIGLU_ALPHA = 1.702
SWIGLU_LIMIT = 7.0
ROPE_THETA = 10000.0
LN_EPS = 1e-5
RMS_EPS = 1e-6
NEG_BIG = -1e30
DN_ALPHA = (2 * DEPTH) ** 0.25
QKV_W = 3 * A_GROUPS * A_W
OFF_B = QKV_W
OFF_C = OFF_B + B_COLS
OFF_G = OFF_C + 4 * C_W
N_IN = OFF_G + 3 * D_MODEL

LANES = 128
SUBLANES = 8
MXU_N = 256
VMEM_LIMIT = 56 * 1024 * 1024

RWKV_CHUNK = 64
HGRN_CHUNK = 64
MOE_BM = 256


def _cparams(sem):
    return pltpu.CompilerParams(dimension_semantics=sem, vmem_limit_bytes=VMEM_LIMIT)


def _bdot(a, b):
    return jnp.dot(a.astype(BF16), b.astype(BF16), preferred_element_type=F32)


def _bdot_nt(a, b):
    return lax.dot_general(a.astype(BF16), b.astype(BF16), (((1,), (1,)), ((), ())), preferred_element_type=F32)


def _bdot_tn(a, b):
    return lax.dot_general(a.astype(BF16), b.astype(BF16), (((0,), (0,)), ((), ())), preferred_element_type=F32)


def _split3(x):
    hi = x.astype(BF16)
    r1 = x - hi.astype(F32)
    mid = r1.astype(BF16)
    lo = (r1 - mid.astype(F32)).astype(BF16)
    return hi, mid, lo


def _dot_exact_lhs(sel, x):
    sel = sel.astype(BF16)
    hi, mid, lo = _split3(x)
    out = jnp.dot(sel, lo, preferred_element_type=F32)
    out = out + jnp.dot(sel, mid, preferred_element_type=F32)
    return out + jnp.dot(sel, hi, preferred_element_type=F32)


def _dot3(a, b):
    a_hi = a.astype(BF16)
    a_lo = (a - a_hi.astype(F32)).astype(BF16)
    b_hi = b.astype(BF16)
    b_lo = (b - b_hi.astype(F32)).astype(BF16)
    out = jnp.dot(a_lo, b_hi, preferred_element_type=F32)
    out = out + jnp.dot(a_hi, b_lo, preferred_element_type=F32)
    return out + jnp.dot(a_hi, b_hi, preferred_element_type=F32)


def _sigmoid(x):
    return 1.0 / (1.0 + jnp.exp(-x))


def _softplus(x):
    return jnp.maximum(x, 0.0) + jnp.log(1.0 + jnp.exp(-jnp.abs(x)))


def _proj_kernel(x_ref, w_ref, o_ref, xb_scr):
    @pl.when(pl.program_id(1) == 0)
    def _():
        xb_scr[...] = x_ref[...].astype(BF16)

    o_ref[...] = jnp.dot(xb_scr[...], w_ref[...], preferred_element_type=F32)


def _project(x2, w_seg, tm, tn):
    T, D = x2.shape
    n_cols = w_seg.shape[1]
    assert n_cols % tn == 0 and tn % LANES == 0 and T % tm == 0
    return pl.pallas_call(
        _proj_kernel,
        name="proj",
        grid=(T // tm, n_cols // tn),
        in_specs=[pl.BlockSpec((tm, D), lambda i, j: (i, 0)),
                  pl.BlockSpec((D, tn), lambda i, j: (0, j))],
        out_specs=pl.BlockSpec((tm, tn), lambda i, j: (i, j)),
        out_shape=jax.ShapeDtypeStruct((T, n_cols), F32),
        scratch_shapes=[pltpu.VMEM((tm, D), BF16)],
        compiler_params=_cparams(("parallel", "arbitrary")),
    )(x2, w_seg)


def _qkv_rope_kernel(x_ref, w_ref, cos_ref, sin_ref, o_ref, xb_scr):
    j = pl.program_id(1)

    @pl.when(j == 0)
    def _():
        xb_scr[...] = x_ref[...].astype(BF16)

    acc = jnp.dot(xb_scr[...], w_ref[...], preferred_element_type=F32)

    @pl.when(j < 2 * A_GROUPS)
    def _():
        lane = lax.broadcasted_iota(jnp.int32, acc.shape, 1)
        first = (lane % A_DH) < (A_DH // 2)
        rot = jnp.where(first, pltpu.roll(acc, A_W - A_DH // 2, 1), pltpu.roll(acc, A_DH // 2, 1))
        o_ref[0] = acc * cos_ref[...] + rot * sin_ref[...]

    @pl.when(j >= 2 * A_GROUPS)
    def _():
        o_ref[0] = acc


def _project_qkv(x2, w_bf16, cos_t, sin_t, tm, period_blocks):
    T, D = x2.shape
    return pl.pallas_call(
        _qkv_rope_kernel,
        name="qkv_rope",
        grid=(T // tm, 3 * A_GROUPS),
        in_specs=[pl.BlockSpec((tm, D), lambda i, j: (i, 0)),
                  pl.BlockSpec((D, A_W), lambda i, j: (0, j)),
                  pl.BlockSpec((tm, A_W), lambda i, j: (i % period_blocks, 0)),
                  pl.BlockSpec((tm, A_W), lambda i, j: (i % period_blocks, 0))],
        out_specs=pl.BlockSpec((1, tm, A_W), lambda i, j: (j, i, 0)),
        out_shape=jax.ShapeDtypeStruct((3 * A_GROUPS, T, A_W), F32),
        scratch_shapes=[pltpu.VMEM((tm, D), BF16)],
        compiler_params=_cparams(("parallel", "arbitrary")),
    )(x2, w_bf16, cos_t, sin_t)


def _rope_tables(pos, rows):
    half = A_DH // 2
    inv = ROPE_THETA ** (-jnp.arange(half, dtype=F32) / half)
    ang = pos.astype(F32)[:, None] * inv[None, :]
    cos = jnp.cos(ang)
    sin = jnp.sin(ang)
    cos_h = jnp.concatenate([cos, cos], -1)
    sin_h = jnp.concatenate([-sin, sin], -1)
    cos_f = jnp.tile(cos_h, (rows // pos.shape[0], A_HEADS))
    sin_f = jnp.tile(sin_h, (rows // pos.shape[0], A_HEADS))
    return cos_f, sin_f


def _attn_prompt_kernel(q_ref, kc_ref, kp_ref, vc_ref, vp_ref, o_ref, lse_ref):
    jb = pl.program_id(2)
    two = 2 * A_BLK
    r2 = lax.broadcasted_iota(jnp.int32, (two, LANES), 0)
    l2 = lax.broadcasted_iota(jnp.int32, (two, LANES), 1)
    qi = r2 % A_BLK
    own_half = (r2 < A_BLK) == (l2 < A_DH)
    mask_c = l2 <= qi
    mask_p = (l2 >= qi) & (jb > 0)
    l1 = lax.broadcasted_iota(jnp.int32, (A_BLK, LANES), 1)
    first = l1 < A_DH
    for p in range(A_HEADS // 2):
        sl = slice(p * LANES, (p + 1) * LANES)
        q = q_ref[0, 0, :, sl] * (A_DH ** -0.5)
        q2 = jnp.where(own_half, jnp.concatenate([q, q], axis=0), 0.0)
        s_c = jnp.where(mask_c, _bdot_nt(q2, kc_ref[0, 0, :, sl]), NEG_BIG)
        s_p = jnp.where(mask_p, _bdot_nt(q2, kp_ref[0, 0, :, sl]), NEG_BIG)
        m = jnp.maximum(jnp.max(s_c, -1, keepdims=True), jnp.max(s_p, -1, keepdims=True))
        p_c = jnp.exp(s_c - m)
        p_p = jnp.exp(s_p - m)
        l = jnp.sum(p_c, -1, keepdims=True) + jnp.sum(p_p, -1, keepdims=True)
        o2 = (_bdot(p_c, vc_ref[0, 0, :, sl]) + _bdot(p_p, vp_ref[0, 0, :, sl])) / l
        lse2 = jnp.broadcast_to(m + jnp.log(l), (two, LANES))
        o_ref[0, 0, :, sl] = jnp.where(first, o2[:A_BLK], o2[A_BLK:])
        lse_ref[0, 0, :, sl] = jnp.where(first, lse2[:A_BLK], lse2[A_BLK:])


def _attn_prompt(q, k, v):
    n, d, R, _ = q.shape
    nb = R // A_BLK
    cur = pl.BlockSpec((1, 1, A_BLK, A_W), lambda b, r, j: (b, r, j, 0))
    prv = pl.BlockSpec((1, 1, A_BLK, A_W), lambda b, r, j: (b, r, jnp.maximum(j - 1, 0), 0))
    shp = jax.ShapeDtypeStruct((n, d, R, A_W), F32)
    return pl.pallas_call(
        _attn_prompt_kernel,
        name="attn_prompt",
        grid=(n, d, nb),
        in_specs=[cur, cur, prv, cur, prv],
        out_specs=[cur, cur],
        out_shape=[shp, shp],
        compiler_params=_cparams(("parallel", "parallel", "arbitrary")),
    )(q, k, k, v, v)


def _window_rows_kernel(x_ref, o_ref):
    xt = x_ref[0].T
    for h in range(A_HEADS):
        o_ref[0, 0, h] = xt[h * A_DH:(h + 1) * A_DH, :]


def _window_rows(qkv, g, N, L):
    keep = min(A_WINDOWS[g], L)
    tw = min(keep, 512)
    assert (L - keep) % tw == 0
    first = lambda n: n * (L // tw) + (L - keep) // tw
    return pl.pallas_call(
        _window_rows_kernel,
        name="window_rows",
        grid=(N, 2, keep // tw),
        in_specs=[pl.BlockSpec((1, tw, A_W), lambda n, kv, j: (A_GROUPS * (1 + kv) + g, first(n) + j, 0))],
        out_specs=pl.BlockSpec((1, 1, A_HEADS, A_DH, tw), lambda n, kv, j: (n, kv, 0, 0, j)),
        out_shape=jax.ShapeDtypeStruct((N, 2, A_HEADS, A_DH, keep), F32),
        compiler_params=_cparams(("parallel", "parallel", "arbitrary")),
    )(qkv)


def _attn_sample_kernel(qkv_ref, c0_ref, c1_ref, c2_ref, o_ref, *, n_new):
    caches = (c0_ref, c1_ref, c2_ref)
    T = n_new
    masks = []
    for g in range(A_GROUPS):
        d = A_DILATIONS[g]
        W = A_WINDOWS[g]
        back_c = lax.broadcasted_iota(jnp.int32, (T, W), 0) - lax.broadcasted_iota(jnp.int32, (T, W), 1)
        back_n = lax.broadcasted_iota(jnp.int32, (T, T), 0) - lax.broadcasted_iota(jnp.int32, (T, T), 1)
        masks.append(((back_c <= 0) & ((back_c & (d - 1)) == 0), (back_n >= 0) & ((back_n & (d - 1)) == 0)))
    def heads(rows):
        return jnp.stack([rows[:, h * A_DH:(h + 1) * A_DH] for h in range(A_HEADS)], 0)

    outs, lses = [], []
    for g in range(A_GROUPS):
        c_ref = caches[g]
        mask_c, mask_n = masks[g]
        q = heads(qkv_ref[g, 0]) * (A_DH ** -0.5)
        s_c = jnp.where(mask_c[None], _bmm(q, c_ref[0], split=False), NEG_BIG)
        s_n = jnp.where(mask_n[None], _bmm(q, heads(qkv_ref[A_GROUPS + g, 0]), "nt", split=False), NEG_BIG)
        m = jnp.maximum(jnp.max(s_c, -1, keepdims=True), jnp.max(s_n, -1, keepdims=True))
        p_c = jnp.exp(s_c - m)
        p_n = jnp.exp(s_n - m)
        l = jnp.sum(p_c, -1, keepdims=True) + jnp.sum(p_n, -1, keepdims=True)
        o = (_bmm(p_c, c_ref[1], "nt", split=False)
             + _bmm(p_n, heads(qkv_ref[2 * A_GROUPS + g, 0]), split=False))
        outs.append(o / l)
        lses.append(m + jnp.log(l))
    mx = jnp.maximum(jnp.maximum(lses[0], lses[1]), lses[2])
    ws = [jnp.exp(x - mx) for x in lses]
    merged = (ws[0] * outs[0] + ws[1] * outs[1] + ws[2] * outs[2]) / (ws[0] + ws[1] + ws[2])
    for h in range(A_HEADS):
        o_ref[0, :, h * A_DH:(h + 1) * A_DH] = merged[h]


def _attn_sample(qkv4, caches_t, layer):
    _, N, T, _ = qkv4.shape
    in_specs = [pl.BlockSpec((3 * A_GROUPS, 1, T, A_W), lambda n: (0, n, 0, 0))]
    for g in range(A_GROUPS):
        assert caches_t[g].shape[2:] == (2, A_HEADS, A_DH, A_WINDOWS[g])
        in_specs.append(pl.BlockSpec((None, None, 2, A_HEADS, A_DH, A_WINDOWS[g]),
                                     lambda n: (layer, n, 0, 0, 0, 0)))
    return pl.pallas_call(
        functools.partial(_attn_sample_kernel, n_new=T),
        name="attn_sample",
        grid=(N,),
        in_specs=in_specs,
        out_specs=pl.BlockSpec((1, T, A_W), lambda n: (n, 0, 0)),
        out_shape=jax.ShapeDtypeStruct((N, T, A_W), F32),
        compiler_params=_cparams(("parallel",)),
    )(qkv4, *caches_t)


_DIMS = {"nn": (((1,), (0,)), ((), ())), "nt": (((1,), (1,)), ((), ())), "tn": (((0,), (0,)), ((), ()))}


def _mm(a, b, mode="nn"):
    dn = _DIMS[mode]
    a_hi = a.astype(BF16)
    a_lo = (a - a_hi.astype(F32)).astype(BF16)
    b_hi = b.astype(BF16)
    b_lo = (b - b_hi.astype(F32)).astype(BF16)
    out = lax.dot_general(a_lo, b_hi, dn, preferred_element_type=F32)
    out = out + lax.dot_general(a_hi, b_lo, dn, preferred_element_type=F32)
    return out + lax.dot_general(a_hi, b_hi, dn, preferred_element_type=F32)


def _tri_incl(c):
    t = lax.broadcasted_iota(jnp.int32, (c, c), 0)
    s = lax.broadcasted_iota(jnp.int32, (c, c), 1)
    return t, s


SOLVE_BLOCK = 8


_BDIMS = {"nn": (((2,), (1,)), ((0,), (0,))), "nt": (((2,), (2,)), ((0,), (0,))), "tn": (((1,), (1,)), ((0,), (0,)))}


def _bmm(a, b, mode="nn", split=True):
    dn = _BDIMS[mode]
    a_hi = a.astype(BF16)
    if not split:
        return lax.dot_general(a_hi, b.astype(BF16), dn, preferred_element_type=F32)
    a_lo = (a - a_hi.astype(F32)).astype(BF16)
    b_hi = b.astype(BF16)
    b_lo = (b - b_hi.astype(F32)).astype(BF16)
    out = lax.dot_general(a_lo, b_hi, dn, preferred_element_type=F32)
    out = out + lax.dot_general(a_hi, b_lo, dn, preferred_element_type=F32)
    return out + lax.dot_general(a_hi, b_hi, dn, preferred_element_type=F32)


def _solve_unit_lower(lm, z, C):
    blk = min(SOLVE_BLOCK, C)
    done = []
    for j in range(C // blk):
        r0 = j * blk
        zj = z[:, r0:r0 + blk]
        if j > 0:
            zj = zj - _bmm(lm[:, r0:r0 + blk, :r0], jnp.concatenate(done, 1), split=False)
        ljj = lm[:, r0:r0 + blk, r0:r0 + blk]
        for s in range(blk - 1):
            zj = zj - ljj[:, :, s:s + 1] * zj[:, s:s + 1, :]
        done.append(zj)
    return jnp.concatenate(done, 1)


def _rwkv_kernel(bc_ref, bprev_ref, s0_ref, mu_ref, w0_ref, a0_ref, kkp_ref, ka_ref, w2_ref, a2_ref, g2_ref,
                 rk_ref, lng_ref, lnb_ref, o_ref, sout_ref,
                 prev_scr, s_scr, h_gam, h_rt, h_kt, h_v, h_kk, h_ebp, h_aenb, h_rk, h_g, h_y, *, C, nb):
    c = pl.program_id(1)
    H, E = B_HEADS, B_DH
    R = nb * C
    B = nb * H

    @pl.when(c == 0)
    def _():
        prev_scr[...] = bprev_ref[...]
        s_scr[...] = s0_ref[...].reshape(B, E, E)

    bc3 = bc_ref[...]
    bc = bc3.reshape(R, B_COLS)
    step = lax.broadcasted_iota(jnp.int32, (R, B_COLS), 0) % C
    carried = jnp.broadcast_to(prev_scr[...], (nb, C, B_COLS)).reshape(R, B_COLS)
    prev = jnp.where(step == 0, carried, pltpu.roll(bc, 1, 0))
    prev_scr[...] = bc3[:, C - 1:C, :]
    mixed = bc + mu_ref[...] * (prev - bc)
    r = mixed[:, 0:B_W]
    k = mixed[:, B_W:2 * B_W]
    v = mixed[:, 2 * B_W:3 * B_W]
    o1 = 3 * B_W
    wl = mixed[:, o1:o1 + B_W_LORA]
    al = mixed[:, o1 + B_W_LORA:o1 + B_W_LORA + B_A_LORA]
    gl = mixed[:, o1 + B_W_LORA + B_A_LORA:]
    w_raw = -_softplus(-(w0_ref[...] + _mm(jnp.tanh(wl), w2_ref[...]))) - 0.5
    lw = -jnp.exp(w_raw)
    a = _sigmoid(a0_ref[...] + _mm(al, a2_ref[...]))
    gate = _mm(_sigmoid(gl), g2_ref[...])
    kkr = k * kkp_ref[...]
    kb = k * (1.0 + (a - 1.0) * ka_ref[...])
    t_r, s_r = _tri_incl(R)
    b = _dot_exact_lhs((s_r <= t_r) & ((s_r // C) == (t_r // C)), lw)
    eb = jnp.exp(b)
    enb = jnp.exp(-b)
    parts = (r * eb, kb * enb, v, kkr, jnp.exp(b - lw), a * enb, r * kb * rk_ref[...], gate)
    for h in range(H):
        sl = slice(h * E, (h + 1) * E)
        for dst, src in zip((h_rt, h_kt, h_v, h_kk, h_ebp, h_aenb, h_rk, h_g), parts):
            dst[:, h] = src[:, sl].reshape(nb, C, E)
        h_gam[:, h] = eb[:, sl].reshape(nb, C, E)[:, C - 1:C, :]

    def batch(ref):
        return ref[...].reshape((B,) + ref.shape[2:])

    t_i, s_i = _tri_incl(C)
    strict = (s_i < t_i)[None]
    incl = (s_i <= t_i)[None]
    eye = (lax.broadcasted_iota(jnp.int32, (E, E), 0) == lax.broadcasted_iota(jnp.int32, (E, E), 1))[None]
    rt, kt, v3, gam = batch(h_rt), batch(h_kt), batch(h_v), batch(h_gam)
    kk = batch(h_kk)
    kk = kk * lax.rsqrt(jnp.maximum(jnp.sum(kk * kk, -1, keepdims=True), 1e-24))
    at = kk * batch(h_ebp)
    bt = kk * batch(h_aenb)
    x = _bmm(jnp.concatenate([at, rt], 1), jnp.concatenate([bt, kt], 1), "nt", split=False)
    lm = jnp.where(strict, x[:, :C, :C], 0.0)
    mk = jnp.where(strict, x[:, :C, C:], 0.0)
    arb = jnp.where(incl, x[:, C:, :C], 0.0)
    ark = jnp.where(incl, x[:, C:, C:], 0.0)
    z = _solve_unit_lower(lm, jnp.concatenate([at, _bmm(mk, v3, split=False)], 2), C)
    qy = jnp.concatenate([rt, _bmm(ark, v3, split=False)], 2) - _bmm(arb, z, split=False)
    top = jnp.where(eye, gam, 0.0)
    gd = (jnp.concatenate([top, _bmm(v3, kt * gam, "tn", split=False)], 1)
          - _bmm(z, bt * gam, "tn", split=False))
    s_all = s_scr[...]
    y = _bmm(qy[:, :, :E], s_all, "nt", split=False) + qy[:, :, E:]
    s_scr[...] = _bmm(s_all, gd[:, :E]) + gd[:, E:]
    mu_y = jnp.mean(y, -1, keepdims=True)
    var = jnp.mean(jnp.square(y - mu_y), -1, keepdims=True)
    lng = jnp.concatenate([lng_ref[...]] * nb, 0)
    lnb = jnp.concatenate([lnb_ref[...]] * nb, 0)
    yn = (y - mu_y) * lax.rsqrt(var + B_GN_EPS) * lng + lnb
    bonus = jnp.sum(batch(h_rk), -1, keepdims=True) * v3
    h_y[...] = ((yn + bonus) * batch(h_g)).reshape(nb, H, C, E)
    for h in range(H):
        o_ref[:, :, h * E:(h + 1) * E] = h_y[:, h]

    @pl.when(c == pl.num_programs(1) - 1)
    def _():
        sout_ref[...] = s_scr[...].reshape(nb, H, E, E)


def _rwkv(bcols, b_prev, s0, p, C, nb):
    N, L, _ = bcols.shape
    H, E = B_HEADS, B_DH
    assert N % nb == 0 and L % C == 0
    row = lambda a: a.reshape(1, -1)
    per_head = lambda a: a.reshape(H, 1, E)
    full = lambda shape: pl.BlockSpec(shape, lambda n, c: (0,) * len(shape))
    hs = lambda: pltpu.VMEM((nb, H, C, E), F32)
    return pl.pallas_call(
        functools.partial(_rwkv_kernel, C=C, nb=nb),
        name="rwkv",
        grid=(N // nb, L // C),
        in_specs=[pl.BlockSpec((nb, C, B_COLS), lambda n, c: (n, c, 0)),
                  pl.BlockSpec((nb, 1, B_COLS), lambda n, c: (n, 0, 0)),
                  pl.BlockSpec((nb, H, E, E), lambda n, c: (n, 0, 0, 0)),
                  full((1, B_COLS)), full((1, B_W)), full((1, B_W)), full((1, B_W)), full((1, B_W)),
                  full((B_W_LORA, B_W)), full((B_A_LORA, B_W)), full((B_G_LORA, B_W)),
                  full((1, B_W)), full((H, 1, E)), full((H, 1, E))],
        out_specs=[pl.BlockSpec((nb, C, B_W), lambda n, c: (n, c, 0)),
                   pl.BlockSpec((nb, H, E, E), lambda n, c: (n, 0, 0, 0))],
        out_shape=[jax.ShapeDtypeStruct((N, L, B_W), F32), jax.ShapeDtypeStruct((N, H, E, E), F32)],
        scratch_shapes=[pltpu.VMEM((nb, 1, B_COLS), F32), pltpu.VMEM((nb * H, E, E), F32),
                        pltpu.VMEM((nb, H, 1, E), F32)] + [hs() for _ in range(9)],
        compiler_params=_cparams(("parallel", "arbitrary")),
    )(bcols, b_prev.reshape(N, 1, B_COLS), s0, row(p['b_mu']), row(p['b_w0']), row(p['b_a0']), row(p['b_kk']),
      row(p['b_ka']), p['b_w2'], p['b_a2'], p['b_g2'], row(p['b_rk']), per_head(p['b_lnx_g']),
      per_head(p['b_lnx_b']))


def _hgrn_levels(C):
    return int(math.log2(C))


def _hgrn_kernel(ch_ref, h0_ref, lb_ref, cng_ref, o_ref, hout_ref,
                 h_scr, hq, hk, hi, hb, hgt, hbp, ho, *, C, nb):
    c = pl.program_id(1)
    H, E = C_HEADS, C_DK
    n_lv = _hgrn_levels(C)
    R = nb * C
    B = nb * H

    @pl.when(c == 0)
    def _():
        h_scr[...] = h0_ref[...].reshape(B, E, E)

    ch = ch_ref[...].reshape(R, 4 * C_W)
    cq = ch[:, 0:C_W]
    z = ch[:, C_W:2 * C_W]
    ci = ch[:, 2 * C_W:3 * C_W]
    cg = ch[:, 3 * C_W:4 * C_W]
    lb = lb_ref[...]
    loglb = jnp.log(jnp.maximum(lb, C_LB_FLOOR))
    log_f = -_softplus(-z) + _softplus(loglb - z)
    kc = (1.0 - lb) * _sigmoid(-z)
    qc = cq * _sigmoid(cq)
    gt = cg * _sigmoid(cg)
    t_r, s_r = _tri_incl(R)
    b = _dot_exact_lhs((s_r <= t_r) & ((s_r // C) == (t_r // C)), log_f)
    bps = []
    for lv in range(n_lv):
        m = 1 << lv
        sel = s_r == (t_r - (t_r % (2 * m)) + m - 1)
        bps.append(_dot_exact_lhs(sel, b))
    for h in range(H):
        sl = slice(h * E, (h + 1) * E)
        for dst, src in zip((hq, hk, hi, hb, hgt), (qc, kc, ci, b, gt)):
            dst[:, h] = src[:, sl].reshape(nb, C, E)
        for lv in range(n_lv):
            hbp[lv, :, h] = bps[lv][:, sl].reshape(nb, C, E)

    def batch(x):
        return x.reshape((B,) + x.shape[2:])

    t_i, s_i = _tri_incl(C)
    eye = (lax.broadcasted_iota(jnp.int32, (E, E), 0) == lax.broadcasted_iota(jnp.int32, (E, E), 1))[None]
    q3, k3, i3, b3 = batch(hq[...]), batch(hk[...]), batch(hi[...]), batch(hb[...])
    att = jnp.where((t_i == s_i)[None], _bmm(q3, k3, "nt", split=False), 0.0)
    for lv in range(n_lv):
        m = 1 << lv
        e = jnp.exp(-jnp.abs(b3 - batch(hbp[lv])))
        keep = (((t_i // (2 * m)) == (s_i // (2 * m))) & ((t_i % (2 * m)) >= m) & ((s_i % (2 * m)) < m))[None]
        att = att + jnp.where(keep, _bmm(q3 * e, k3 * e, "nt", split=False), 0.0)
    h_m = h_scr[...]
    o = _bmm(q3 * jnp.exp(b3), h_m, split=False) + _bmm(att, i3, split=False)
    b_last = b3[:, C - 1:C, :]
    dg = jnp.where(eye, jnp.exp(b_last), 0.0)
    kdec = k3 * jnp.exp(b_last - b3)
    h_scr[...] = _bmm(jnp.concatenate([dg, kdec], 1), jnp.concatenate([h_m, i3], 1), "tn")
    o = o * lax.rsqrt(jnp.mean(jnp.square(o), -1, keepdims=True) + RMS_EPS) * cng_ref[...]
    ho[...] = (o * batch(hgt[...])).reshape(nb, H, C, E)
    for h in range(H):
        o_ref[:, :, h * E:(h + 1) * E] = ho[:, h]

    @pl.when(c == pl.num_programs(1) - 1)
    def _():
        hout_ref[...] = h_scr[...].reshape(nb, H, E, E)


def _hgrn(ch, h0, lb, c_norm_g, C, nb):
    N, L, _ = ch.shape
    H, E = C_HEADS, C_DK
    n_lv = _hgrn_levels(C)
    assert N % nb == 0 and L % C == 0
    full = lambda shape: pl.BlockSpec(shape, lambda n, c: (0,) * len(shape))
    hs = lambda: pltpu.VMEM((nb, H, C, E), F32)
    return pl.pallas_call(
        functools.partial(_hgrn_kernel, C=C, nb=nb),
        name="hgrn",
        grid=(N // nb, L // C),
        in_specs=[pl.BlockSpec((nb, C, 4 * C_W), lambda n, c: (n, c, 0)),
                  pl.BlockSpec((nb, H, E, E), lambda n, c: (n, 0, 0, 0)),
                  full((1, C_W)), full((1, E))],
        out_specs=[pl.BlockSpec((nb, C, C_W), lambda n, c: (n, c, 0)),
                   pl.BlockSpec((nb, H, E, E), lambda n, c: (n, 0, 0, 0))],
        out_shape=[jax.ShapeDtypeStruct((N, L, C_W), F32), jax.ShapeDtypeStruct((N, H, E, E), F32)],
        scratch_shapes=[pltpu.VMEM((nb * H, E, E), F32), hs(), hs(), hs(), hs(), hs(),
                        pltpu.VMEM((n_lv, nb, H, C, E), F32), hs()],
        compiler_params=_cparams(("parallel", "arbitrary")),
    )(ch, h0, lb.reshape(1, C_W), c_norm_g.reshape(1, E))


def _layer_norm(v, g, b):
    mu = jnp.mean(v, -1, keepdims=True)
    var = jnp.mean(jnp.square(v - mu), -1, keepdims=True)
    return (v - mu) * lax.rsqrt(var + LN_EPS) * g + b


def _merge_kernel(*refs, n_attn, tm):
    attn = refs[:n_attn]
    (ob_ref, oc_ref, gates_ref, x_ref, wa_ref, wb_ref, wc_ref, wo_ref, lng_ref, lnb_ref, wr_ref, br_ref,
     x1_ref, idx_ref, gate_ref, rank_ref, cnt_ref, run_scr) = refs[n_attn:]
    i = pl.program_id(0)

    @pl.when(i == 0)
    def _():
        run_scr[...] = jnp.zeros_like(run_scr)

    if n_attn == 1:
        o_a = attn[0][...]
    else:
        o0, l0, o1, l1, o2, l2 = (r[...] for r in attn)
        mx = jnp.maximum(jnp.maximum(l0, l1), l2)
        w0, w1, w2 = jnp.exp(l0 - mx), jnp.exp(l1 - mx), jnp.exp(l2 - mx)
        o_a = (w0 * o0 + w1 * o1 + w2 * o2) / (w0 + w1 + w2)
    gates = gates_ref[...]
    merged = (_sigmoid(gates[:, 0:D_MODEL]) * _bdot(o_a, wa_ref[...])
              + _sigmoid(gates[:, D_MODEL:2 * D_MODEL]) * _bdot(ob_ref[...], wb_ref[...])
              + _sigmoid(gates[:, 2 * D_MODEL:]) * _bdot(oc_ref[...], wc_ref[...]))
    mix = _bdot(merged, wo_ref[...])
    x1 = _layer_norm(DN_ALPHA * x_ref[...] + mix, lng_ref[...], lnb_ref[...])
    x1_ref[...] = x1

    logits = _bdot(x1, wr_ref[...]) + br_ref[...]
    lane = lax.broadcasted_iota(jnp.int32, (tm, LANES), 1)
    cur = jnp.where(lane < N_EXPERTS, logits, -jnp.inf)
    vals, idxs, sels = [], [], []
    for _ in range(TOP_K):
        mx = jnp.max(cur, -1, keepdims=True)
        idx = jnp.min(jnp.where(cur == mx, lane, LANES), -1, keepdims=True)
        sel = lane == idx
        vals.append(mx)
        idxs.append(idx)
        sels.append(sel)
        cur = jnp.where(sel, -jnp.inf, cur)
    es = [jnp.exp(v - vals[0]) for v in vals]
    den = es[0] + es[1] + es[2] + es[3]
    mask = jnp.zeros((tm, LANES), F32)
    for sel in sels:
        mask = jnp.where(sel, 1.0, mask)
    r_i = lax.broadcasted_iota(jnp.int32, (tm, tm), 0)
    c_i = lax.broadcasted_iota(jnp.int32, (tm, tm), 1)
    excl = jnp.dot((c_i < r_i).astype(BF16), mask.astype(BF16), preferred_element_type=F32) + run_scr[...]
    idx_o = jnp.zeros((tm, LANES), jnp.int32)
    gate_o = jnp.zeros((tm, LANES), F32)
    rank_o = jnp.zeros((tm, LANES), F32)
    for j in range(TOP_K):
        rank_j = jnp.sum(jnp.where(sels[j], excl, 0.0), -1, keepdims=True)
        idx_o = jnp.where(lane == j, idxs[j], idx_o)
        gate_o = jnp.where(lane == j, es[j] / den, gate_o)
        rank_o = jnp.where(lane == j, rank_j, rank_o)
    idx_ref[...] = idx_o
    gate_ref[...] = gate_o
    rank_ref[...] = rank_o.astype(jnp.int32)
    run_scr[...] = run_scr[...] + jnp.sum(mask, axis=0, keepdims=True)
    cnt_ref[...] = jnp.broadcast_to(run_scr[...], cnt_ref.shape)


def _merge(attn, o_b, o_c, gates, x2, wts, tm):
    T = x2.shape[0]
    n_attn = len(attn)
    rowspec = lambda w: pl.BlockSpec((tm, w), lambda i: (i, 0))
    full = lambda a: pl.BlockSpec(a.shape, lambda i: (0,) * a.ndim)
    wlist = [wts['w_br_a'], wts['w_br_b'], wts['w_br_c'], wts['w_out'], wts['ln_g0'], wts['ln_b0'],
             wts['w_router'], wts['b_router']]
    return pl.pallas_call(
        functools.partial(_merge_kernel, n_attn=n_attn, tm=tm),
        name="merge",
        grid=(T // tm,),
        in_specs=[rowspec(A_W)] * n_attn + [rowspec(B_W), rowspec(C_W), rowspec(3 * D_MODEL), rowspec(D_MODEL)]
                 + [full(w) for w in wlist],
        out_specs=[rowspec(D_MODEL), rowspec(LANES), rowspec(LANES), rowspec(LANES),
                   pl.BlockSpec((SUBLANES, LANES), lambda i: (0, 0))],
        out_shape=[jax.ShapeDtypeStruct((T, D_MODEL), F32), jax.ShapeDtypeStruct((T, LANES), jnp.int32),
                   jax.ShapeDtypeStruct((T, LANES), F32), jax.ShapeDtypeStruct((T, LANES), jnp.int32),
                   jax.ShapeDtypeStruct((SUBLANES, LANES), F32)],
        scratch_shapes=[pltpu.VMEM((1, LANES), F32)],
        compiler_params=_cparams(("arbitrary",)),
    )(*attn, o_b, o_c, gates, x2, *wlist)


def _dispatch_kernel(dest_ref, x_ref, xs_in_ref, xs_ref, sem, *, td):
    del xs_in_ref

    n_rows = td * TOP_K

    def issue(t, carry):
        for j in range(TOP_K):
            pltpu.make_async_copy(x_ref.at[pl.ds(t, 1)], xs_ref.at[pl.ds(dest_ref[0, 0, t * TOP_K + j], 1)],
                                  sem).start()
        return carry

    lax.fori_loop(0, td, issue, 0, unroll=2)
    pltpu.make_async_copy(xs_ref.at[pl.ds(0, n_rows)], xs_ref.at[pl.ds(0, n_rows)], sem).wait()


def _dispatch(x1, dest, n_slots, td):
    T = x1.shape[0]
    zeros = jnp.zeros((n_slots, D_MODEL), F32)
    return pl.pallas_call(
        functools.partial(_dispatch_kernel, td=td),
        name="dispatch",
        grid=(T // td,),
        in_specs=[pl.BlockSpec((1, 1, td * TOP_K), lambda i: (i, 0, 0), memory_space=pltpu.SMEM),
                  pl.BlockSpec((td, D_MODEL), lambda i: (i, 0)),
                  pl.BlockSpec(memory_space=pl.ANY)],
        out_specs=pl.BlockSpec(memory_space=pl.ANY),
        out_shape=jax.ShapeDtypeStruct((n_slots, D_MODEL), F32),
        scratch_shapes=[pltpu.SemaphoreType.DMA(())],
        input_output_aliases={2: 0},
        compiler_params=_cparams(("arbitrary",)),
    )(dest.reshape(T // td, 1, td * TOP_K), x1, zeros)


def _expert_kernel(be_ref, nu_ref, xs_ref, wu_ref, bu_ref, wd_ref, bd_ref, o_ref):
    b = pl.program_id(0)

    @pl.when(b < nu_ref[0])
    def _():
        h = jnp.dot(xs_ref[...].astype(BF16), wu_ref[0], preferred_element_type=F32) + bu_ref[0]
        h_glu = jnp.minimum(h[:, :D_FF], SWIGLU_LIMIT)
        h_lin = jnp.clip(h[:, D_FF:], -SWIGLU_LIMIT, SWIGLU_LIMIT)
        act = h_glu * _sigmoid(SWIGLU_ALPHA * h_glu) * (h_lin + 1.0)
        o_ref[...] = jnp.dot(act.astype(BF16), wd_ref[0], preferred_element_type=F32) + bd_ref[0]

    @pl.when(b >= nu_ref[0])
    def _():
        o_ref[...] = jnp.zeros_like(o_ref)


def _experts(xs, block_expert, n_used, w_up, b_up, w_down, b_down):
    n_slots = xs.shape[0]
    n_blocks = n_slots // MOE_BM
    grid_spec = pltpu.PrefetchScalarGridSpec(
        num_scalar_prefetch=2,
        grid=(n_blocks,),
        in_specs=[pl.BlockSpec((MOE_BM, D_MODEL), lambda b, be, nu: (b, 0)),
                  pl.BlockSpec((1, D_MODEL, 2 * D_FF), lambda b, be, nu: (be[b], 0, 0)),
                  pl.BlockSpec((1, 1, 2 * D_FF), lambda b, be, nu: (be[b], 0, 0)),
                  pl.BlockSpec((1, D_FF, D_MODEL), lambda b, be, nu: (be[b], 0, 0)),
                  pl.BlockSpec((1, 1, D_MODEL), lambda b, be, nu: (be[b], 0, 0))],
        out_specs=pl.BlockSpec((MOE_BM, D_MODEL), lambda b, be, nu: (b, 0)),
    )
    return pl.pallas_call(
        _expert_kernel,
        name="experts",
        grid_spec=grid_spec,
        out_shape=jax.ShapeDtypeStruct((n_slots, D_MODEL), F32),
        compiler_params=_cparams(("arbitrary",)),
    )(block_expert, n_used, xs, w_up, b_up.reshape(N_EXPERTS, 1, 2 * D_FF), w_down,
      b_down.reshape(N_EXPERTS, 1, D_MODEL))


def _combine_kernel(dest_ref, gate_ref, x1_ref, eo_ref, lng_ref, lnb_ref, o_ref, buf, sem, *, tc):
    n_rows = tc * TOP_K

    def issue(t, carry):
        for j in range(TOP_K):
            pltpu.make_async_copy(eo_ref.at[pl.ds(dest_ref[0, 0, t * TOP_K + j], 1)],
                                  buf.at[pl.ds(j * tc + t, 1)], sem).start()
        return carry

    lax.fori_loop(0, tc, issue, 0, unroll=2)
    pltpu.make_async_copy(eo_ref.at[pl.ds(0, n_rows)], buf, sem).wait()
    gate = gate_ref[...]
    y = gate[:, 0:1] * buf[0:tc]
    for j in range(1, TOP_K):
        y = y + gate[:, j:j + 1] * buf[j * tc:(j + 1) * tc]
    o_ref[...] = _layer_norm(DN_ALPHA * x1_ref[...] + y, lng_ref[...], lnb_ref[...])


def _combine(dest, gate, x1, eo, ln_g, ln_b, tc):
    T = x1.shape[0]
    full = lambda a: pl.BlockSpec(a.shape, lambda i: (0,) * a.ndim)
    return pl.pallas_call(
        functools.partial(_combine_kernel, tc=tc),
        name="combine",
        grid=(T // tc,),
        in_specs=[pl.BlockSpec((1, 1, tc * TOP_K), lambda i: (i, 0, 0), memory_space=pltpu.SMEM),
                  pl.BlockSpec((tc, LANES), lambda i: (i, 0)),
                  pl.BlockSpec((tc, D_MODEL), lambda i: (i, 0)),
                  pl.BlockSpec(memory_space=pl.ANY), full(ln_g), full(ln_b)],
        out_specs=pl.BlockSpec((tc, D_MODEL), lambda i: (i, 0)),
        out_shape=jax.ShapeDtypeStruct((T, D_MODEL), F32),
        scratch_shapes=[pltpu.VMEM((TOP_K * tc, D_MODEL), F32), pltpu.SemaphoreType.DMA(())],
        compiler_params=_cparams(("arbitrary",)),
    )(dest.reshape(T // tc, 1, tc * TOP_K), gate, x1, eo, ln_g, ln_b)


def _moe(x1, top_idx, top_gate, top_rank, counts, wts):
    T = x1.shape[0]
    cnt = counts[0, :N_EXPERTS].astype(jnp.int32)
    padded = (cnt + MOE_BM - 1) // MOE_BM * MOE_BM
    pend = jnp.cumsum(padded)
    pstart = pend - padded
    dest = (pstart[top_idx[:, :TOP_K]] + top_rank[:, :TOP_K]).astype(jnp.int32)
    n_blocks = T * TOP_K // MOE_BM + N_EXPERTS
    block_start = jnp.arange(n_blocks, dtype=jnp.int32) * MOE_BM
    block_expert = jnp.minimum(jnp.sum(pend[None, :] <= block_start[:, None], axis=1), N_EXPERTS - 1).astype(jnp.int32)
    n_used = (pend[-1:] // MOE_BM).astype(jnp.int32)
    xs = _dispatch(x1, dest, n_blocks * MOE_BM, min(256, T))
    eo = _experts(xs, block_expert, n_used, wts['w_up'], wts['b_up'], wts['w_down'], wts['b_down'])
    return _combine(dest, top_gate, x1, eo, wts['ln_g1'], wts['ln_b1'], min(128, T))


def _hybrid_layer(x, pos, caches, layer, b_shift, b_wkv, c_st, lb, wts):
    N, L, D = x.shape
    T = N * L
    x2 = x.reshape(T, D)
    tm = min(1024, T)
    rows = max(L, tm)
    cos_t, sin_t = _rope_tables(pos, rows)
    qkv = _project_qkv(x2, wts['w_qkv'], cos_t, sin_t, tm, rows // tm)
    bcols = _project(x2, wts['w_b'], tm, B_COLS // 2).reshape(N, L, B_COLS)
    ch = _project(x2, wts['w_c'], tm, 2 * C_W).reshape(N, L, 4 * C_W)
    gates = _project(x2, wts['w_g'], tm, 3 * D_MODEL // 2)

    a_new = []
    if caches is None:
        attn = []
        for g in range(A_GROUPS):
            d = A_DILATIONS[g]
            to_res = lambda t: t.reshape(N, L // d, d, A_W).transpose(0, 2, 1, 3)
            o, lse = _attn_prompt(to_res(qkv[g]), to_res(qkv[A_GROUPS + g]), to_res(qkv[2 * A_GROUPS + g]))
            from_res = lambda t: t.transpose(0, 2, 1, 3).reshape(T, A_W)
            attn += [from_res(o), from_res(lse)]
            a_new.append(_window_rows(qkv, g, N, L))
    else:
        qkv5 = qkv.reshape(3 * A_GROUPS, N, L, A_HEADS, A_DH)
        attn = [_attn_sample(qkv.reshape(3 * A_GROUPS, N, L, A_W), caches, layer).reshape(T, A_W)]
        for g in range(A_GROUPS):
            a_new.append(jnp.stack([qkv5[A_GROUPS + g], qkv5[2 * A_GROUPS + g]], axis=2))

    seqs_per_step = max(1, min(N, RWKV_CHUNK // L))
    o_b, b_s = _rwkv(bcols, b_shift, b_wkv, wts, min(RWKV_CHUNK, L), seqs_per_step)
    o_c, c_s = _hgrn(ch, c_st, lb, wts['c_norm_g'], min(HGRN_CHUNK, L), seqs_per_step)
    x1, top_idx, top_gate, top_rank, counts = _merge(
        attn, o_b.reshape(T, B_W), o_c.reshape(T, C_W), gates, x2, wts, min(512, T))
    x_out = _moe(x1, top_idx, top_gate, top_rank, counts, wts)
    return x_out.reshape(N, L, D), (a_new, bcols[:, -1], b_s, c_s)


def kernel(x_prompt, x_sample, cache_a0_kv, cache_a1_kv, cache_a2_kv, state_b_shift, state_b_wkv, state_c,
           w_in, b_mu, b_w0, b_w2, b_a0, b_a2, b_g2, b_kk, b_ka, b_rk, b_lnx_g, b_lnx_b, c_lb, c_norm_g,
           w_br_a, w_br_b, w_br_c, w_out, ln_g, ln_b, w_router, b_router, w_up, b_up, w_down, b_down):
    lb_all = jax.nn.softmax(c_lb.astype(F32), axis=0)
    lb_all = jnp.cumsum(lb_all, axis=0) - lb_all[0]
    pad_e = LANES - N_EXPERTS
    layers = []
    for l in range(DEPTH):
        layers.append({
            'w_qkv': w_in[l, :, :OFF_B].astype(BF16), 'w_b': w_in[l, :, OFF_B:OFF_C].astype(BF16),
            'w_c': w_in[l, :, OFF_C:OFF_G].astype(BF16), 'w_g': w_in[l, :, OFF_G:].astype(BF16), 'b_mu': b_mu[l], 'b_w0': b_w0[l], 'b_w2': b_w2[l], 'b_a0': b_a0[l],
            'b_a2': b_a2[l], 'b_g2': b_g2[l], 'b_kk': b_kk[l], 'b_ka': b_ka[l], 'b_rk': b_rk[l],
            'b_lnx_g': b_lnx_g[l], 'b_lnx_b': b_lnx_b[l], 'c_norm_g': c_norm_g[l],
            'w_br_a': w_br_a[l].astype(BF16), 'w_br_b': w_br_b[l].astype(BF16), 'w_br_c': w_br_c[l].astype(BF16),
            'w_out': w_out[l].astype(BF16),
            'ln_g0': ln_g[l, 0].reshape(1, D_MODEL), 'ln_b0': ln_b[l, 0].reshape(1, D_MODEL),
            'ln_g1': ln_g[l, 1].reshape(1, D_MODEL), 'ln_b1': ln_b[l, 1].reshape(1, D_MODEL),
            'w_router': jnp.pad(w_router[l].astype(F32), ((0, 0), (0, pad_e))),
            'b_router': jnp.pad(b_router[l].astype(F32), (0, pad_e)).reshape(1, LANES),
            'w_up': w_up[l].astype(BF16), 'b_up': b_up[l], 'w_down': w_down[l].astype(BF16), 'b_down': b_down[l]})

    def run_group(x, pos, caches, b_shift, b_wkv, c_st):
        new = ([], [], [], [], [], [])
        for l in range(DEPTH):
            x, (a_new, b_last, b_s, c_s) = _hybrid_layer(x, pos, caches, l, b_shift[l], b_wkv[l], c_st[l],
                                                         lb_all[l], layers[l])
            for g in range(A_GROUPS):
                new[g].append(a_new[g])
            new[3].append(b_last)
            new[4].append(b_s)
            new[5].append(c_s)
        return x, [jnp.stack(s) for s in new]

    nb_p, seq = x_prompt.shape[0], x_prompt.shape[1]
    y_prompt, (p_a0, p_a1, p_a2, p_bs, p_bw, p_c) = run_group(
        x_prompt, jnp.arange(seq, dtype=jnp.int32), None,
        jnp.zeros((DEPTH, nb_p, B_COLS), F32),
        jnp.zeros((DEPTH, nb_p, B_HEADS, B_DH, B_DH), F32),
        jnp.zeros((DEPTH, nb_p, C_HEADS, C_DK, C_DV), F32))
    p_a0, p_a1, p_a2 = (jnp.transpose(a, (0, 1, 5, 2, 3, 4)) for a in (p_a0, p_a1, p_a2))
    pos_s = PAST_LEN + jnp.arange(x_sample.shape[1], dtype=jnp.int32)
    caches_t = tuple(jnp.transpose(c, (0, 1, 3, 4, 5, 2)) for c in (cache_a0_kv, cache_a1_kv, cache_a2_kv))
    y_sample, (s_a0, s_a1, s_a2, s_bs, s_bw, s_c) = run_group(
        x_sample, pos_s, caches_t, state_b_shift, state_b_wkv, state_c)
    return (y_prompt, y_sample, p_a0, p_a1, p_a2, p_bs, p_bw, p_c, s_a0, s_a1, s_a2, s_bs, s_bw, s_c)
```

```python
import functools
import math

import numpy as np
import jax
import jax.numpy as jnp
from jax import lax
from jax.experimental import pallas as pl
from jax.experimental.pallas import tpu as pltpu

F32 = jnp.float32
BF16 = jnp.bfloat16

D_MODEL = 1024
DEPTH = 2
PAST_LEN = 2048
A_WINDOWS = (128, 512, 2048)
A_DILATIONS = (1, 4, 16)
A_GROUPS = 3
A_HEADS = 8
A_DH = 64
A_W = A_HEADS * A_DH
A_BLK = 128
B_DH = 64
B_HEADS = 16
B_W = 1024
B_W_LORA = 64
B_A_LORA = 64
B_G_LORA = 128
B_COLS = 3 * B_W + B_W_LORA + B_A_LORA + B_G_LORA
B_GN_EPS = 64e-5
C_DK = 64
C_DV = 64
C_HEADS = 16
C_W = 1024
C_LB_FLOOR = 1e-30
N_EXPERTS = 32
TOP_K = 4
D_FF = 1024
SWIGLU_ALPHA = 1.702
SWIGLU_LIMIT = 7.0
ROPE_THETA = 10000.0
LN_EPS = 1e-5
RMS_EPS = 1e-6
NEG_BIG = -1e30
DN_ALPHA = (2 * DEPTH) ** 0.25
QKV_W = 3 * A_GROUPS * A_W
OFF_B = QKV_W
OFF_C = OFF_B + B_COLS
OFF_G = OFF_C + 4 * C_W
N_IN = OFF_G + 3 * D_MODEL

LANES = 128
SUBLANES = 8
MXU_N = 256
VMEM_LIMIT = 56 * 1024 * 1024

RWKV_CHUNK = 64
HGRN_CHUNK = 64
MOE_BM = 256


def _cparams(sem):
    return pltpu.CompilerParams(dimension_semantics=sem, vmem_limit_bytes=VMEM_LIMIT)


def _bdot(a, b):
    return jnp.dot(a.astype(BF16), b.astype(BF16), preferred_element_type=F32)


def _bdot_nt(a, b):
    return lax.dot_general(a.astype(BF16), b.astype(BF16), (((1,), (1,)), ((), ())), preferred_element_type=F32)


def _bdot_tn(a, b):
    return lax.dot_general(a.astype(BF16), b.astype(BF16), (((0,), (0,)), ((), ())), preferred_element_type=F32)


def _split3(x):
    hi = x.astype(BF16)
    r1 = x - hi.astype(F32)
    mid = r1.astype(BF16)
    lo = (r1 - mid.astype(F32)).astype(BF16)
    return hi, mid, lo


def _dot_exact_lhs(sel, x):
    sel = sel.astype(BF16)
    hi, mid, lo = _split3(x)
    out = jnp.dot(sel, lo, preferred_element_type=F32)
    out = out + jnp.dot(sel, mid, preferred_element_type=F32)
    return out + jnp.dot(sel, hi, preferred_element_type=F32)


def _dot3(a, b):
    a_hi = a.astype(BF16)
    a_lo = (a - a_hi.astype(F32)).astype(BF16)
    b_hi = b.astype(BF16)
    b_lo = (b - b_hi.astype(F32)).astype(BF16)
    out = jnp.dot(a_lo, b_hi, preferred_element_type=F32)
    out = out + jnp.dot(a_hi, b_lo, preferred_element_type=F32)
    return out + jnp.dot(a_hi, b_hi, preferred_element_type=F32)


def _sigmoid(x):
    return 1.0 / (1.0 + jnp.exp(-x))


def _softplus(x):
    return jnp.maximum(x, 0.0) + jnp.log(1.0 + jnp.exp(-jnp.abs(x)))


def _proj_kernel(x_ref, w_ref, o_ref, xb_scr):
    @pl.when(pl.program_id(1) == 0)
    def _():
        xb_scr[...] = x_ref[...].astype(BF16)

    o_ref[...] = jnp.dot(xb_scr[...], w_ref[...], preferred_element_type=F32)


def _project(x2, w_seg, tm, tn):
    T, D = x2.shape
    n_cols = w_seg.shape[1]
    assert n_cols % tn == 0 and tn % LANES == 0 and T % tm == 0
    return pl.pallas_call(
        _proj_kernel,
        name="proj",
        grid=(T // tm, n_cols // tn),
        in_specs=[pl.BlockSpec((tm, D), lambda i, j: (i, 0)),
                  pl.BlockSpec((D, tn), lambda i, j: (0, j))],
        out_specs=pl.BlockSpec((tm, tn), lambda i, j: (i, j)),
        out_shape=jax.ShapeDtypeStruct((T, n_cols), F32),
        scratch_shapes=[pltpu.VMEM((tm, D), BF16)],
        compiler_params=_cparams(("parallel", "arbitrary")),
    )(x2, w_seg)


def _qkv_rope_kernel(x_ref, w_ref, cos_ref, sin_ref, o_ref, xb_scr):
    j = pl.program_id(1)

    @pl.when(j == 0)
    def _():
        xb_scr[...] = x_ref[...].astype(BF16)

    acc = jnp.dot(xb_scr[...], w_ref[...], preferred_element_type=F32)

    @pl.when(j < 2 * A_GROUPS)
    def _():
        lane = lax.broadcasted_iota(jnp.int32, acc.shape, 1)
        first = (lane % A_DH) < (A_DH // 2)
        rot = jnp.where(first, pltpu.roll(acc, A_W - A_DH // 2, 1), pltpu.roll(acc, A_DH // 2, 1))
        o_ref[0] = acc * cos_ref[...] + rot * sin_ref[...]

    @pl.when(j >= 2 * A_GROUPS)
    def _():
        o_ref[0] = acc


def _project_qkv(x2, w_bf16, cos_t, sin_t, tm, period_blocks):
    T, D = x2.shape
    return pl.pallas_call(
        _qkv_rope_kernel,
        name="qkv_rope",
        grid=(T // tm, 3 * A_GROUPS),
        in_specs=[pl.BlockSpec((tm, D), lambda i, j: (i, 0)),
                  pl.BlockSpec((D, A_W), lambda i, j: (0, j)),
                  pl.BlockSpec((tm, A_W), lambda i, j: (i % period_blocks, 0)),
                  pl.BlockSpec((tm, A_W), lambda i, j: (i % period_blocks, 0))],
        out_specs=pl.BlockSpec((1, tm, A_W), lambda i, j: (j, i, 0)),
        out_shape=jax.ShapeDtypeStruct((3 * A_GROUPS, T, A_W), F32),
        scratch_shapes=[pltpu.VMEM((tm, D), BF16)],
        compiler_params=_cparams(("parallel", "arbitrary")),
    )(x2, w_bf16, cos_t, sin_t)


def _rope_tables(pos, rows):
    half = A_DH // 2
    inv = ROPE_THETA ** (-jnp.arange(half, dtype=F32) / half)
    ang = pos.astype(F32)[:, None] * inv[None, :]
    cos = jnp.cos(ang)
    sin = jnp.sin(ang)
    cos_h = jnp.concatenate([cos, cos], -1)
    sin_h = jnp.concatenate([-sin, sin], -1)
    cos_f = jnp.tile(cos_h, (rows // pos.shape[0], A_HEADS))
    sin_f = jnp.tile(sin_h, (rows // pos.shape[0], A_HEADS))
    return cos_f, sin_f


def _attn_dilated_kernel(q_ref, kc_ref, kp_ref, vc_ref, vp_ref, o_ref, lse_ref, *, d, m):
    jb = pl.program_id(1)
    two = 2 * A_BLK
    span = A_BLK * d
    r2 = lax.broadcasted_iota(jnp.int32, (two, LANES), 0)
    l2 = lax.broadcasted_iota(jnp.int32, (two, LANES), 1)
    qi = r2 % A_BLK
    own_half = (r2 < A_BLK) == (l2 < A_DH)
    mask_c = l2 <= qi
    mask_p = l2 >= qi
    first = lax.broadcasted_iota(jnp.int32, (A_BLK, LANES), 1) < A_DH

    def rows(i, r):
        return pl.ds(i * span + r, A_BLK, stride=d) if d > 1 else pl.ds(i * span, A_BLK)

    for i in range(m):
        for r in range(d):
            cur = rows(i, r)
            q = q_ref[cur, :] * (A_DH ** -0.5)
            q2 = jnp.where(own_half, jnp.concatenate([q, q], axis=0), 0.0)
            if i == 0:
                k_prev, v_prev = kp_ref[rows(0, r), :], vp_ref[rows(0, r), :]
                keep_p = mask_p & (jb > 0)
            else:
                k_prev, v_prev = kc_ref[rows(i - 1, r), :], vc_ref[rows(i - 1, r), :]
                keep_p = mask_p
            s_c = jnp.where(mask_c, _bdot_nt(q2, kc_ref[cur, :]), NEG_BIG)
            s_p = jnp.where(keep_p, _bdot_nt(q2, k_prev), NEG_BIG)
            mx = jnp.maximum(jnp.max(s_c, -1, keepdims=True), jnp.max(s_p, -1, keepdims=True))
            p_c = jnp.exp(s_c - mx)
            p_p = jnp.exp(s_p - mx)
            l = jnp.sum(p_c, -1, keepdims=True) + jnp.sum(p_p, -1, keepdims=True)
            o2 = (_bdot(p_c, vc_ref[cur, :]) + _bdot(p_p, v_prev)) / l
            lse2 = jnp.broadcast_to(mx + jnp.log(l), (two, LANES))
            o_ref[cur, :] = jnp.where(first, o2[:A_BLK], o2[A_BLK:])
            lse_ref[cur, :] = jnp.where(first, lse2[:A_BLK], lse2[A_BLK:])


ATTN_ROWS_PER_STEP = 512


def _attn_dilated(qkv, g, N, L):
    d = A_DILATIONS[g]
    T = N * L
    span = A_BLK * d
    m = max(1, ATTN_ROWS_PER_STEP // span)
    blk = span * m
    assert L % blk == 0
    nblk = L // blk
    cur = lambda a: pl.BlockSpec((None, blk, LANES), lambda n, j, p: (a, n * nblk + j, p))
    prv = lambda a: pl.BlockSpec((None, span, LANES),
                                 lambda n, j, p: (a, n * nblk * m + jnp.maximum(j * m - 1, 0), p))
    out = pl.BlockSpec((blk, LANES), lambda n, j, p: (n * nblk + j, p))
    shp = jax.ShapeDtypeStruct((T, A_W), F32)
    return pl.pallas_call(
        functools.partial(_attn_dilated_kernel, d=d, m=m),
        name="attn_dilated",
        grid=(N, nblk, A_HEADS // 2),
        in_specs=[cur(g), cur(A_GROUPS + g), prv(A_GROUPS + g), cur(2 * A_GROUPS + g), prv(2 * A_GROUPS + g)],
        out_specs=[out, out],
        out_shape=[shp, shp],
        compiler_params=_cparams(("parallel", "arbitrary", "arbitrary")),
    )(qkv, qkv, qkv, qkv, qkv)


def _window_rows_kernel(x_ref, o_ref):
    xt = x_ref[0].T
    for h in range(A_HEADS):
        o_ref[0, 0, h] = xt[h * A_DH:(h + 1) * A_DH, :]


def _window_rows(qkv, g, N, L):
    keep = min(A_WINDOWS[g], L)
    tw = min(keep, 512)
    assert (L - keep) % tw == 0
    first = lambda n: n * (L // tw) + (L - keep) // tw
    return pl.pallas_call(
        _window_rows_kernel,
        name="window_rows",
        grid=(N, 2, keep // tw),
        in_specs=[pl.BlockSpec((1, tw, A_W), lambda n, kv, j: (A_GROUPS * (1 + kv) + g, first(n) + j, 0))],
        out_specs=pl.BlockSpec((1, 1, A_HEADS, A_DH, tw), lambda n, kv, j: (n, kv, 0, 0, j)),
        out_shape=jax.ShapeDtypeStruct((N, 2, A_HEADS, A_DH, keep), F32),
        compiler_params=_cparams(("parallel", "parallel", "arbitrary")),
    )(qkv)


def _attn_sample_kernel(qkv_ref, c0_ref, c1_ref, c2_ref, o_ref, *, n_new):
    caches = (c0_ref, c1_ref, c2_ref)
    T = n_new
    masks = []
    for g in range(A_GROUPS):
        d = A_DILATIONS[g]
        W = A_WINDOWS[g]
        back_c = lax.broadcasted_iota(jnp.int32, (T, W), 0) - lax.broadcasted_iota(jnp.int32, (T, W), 1)
        back_n = lax.broadcasted_iota(jnp.int32, (T, T), 0) - lax.broadcasted_iota(jnp.int32, (T, T), 1)
        masks.append(((back_c <= 0) & ((back_c & (d - 1)) == 0), (back_n >= 0) & ((back_n & (d - 1)) == 0)))
    def heads(rows):
        return jnp.stack([rows[:, h * A_DH:(h + 1) * A_DH] for h in range(A_HEADS)], 0)

    outs, lses = [], []
    for g in range(A_GROUPS):
        c_ref = caches[g]
        mask_c, mask_n = masks[g]
        q = heads(qkv_ref[g, 0]) * (A_DH ** -0.5)
        s_c = jnp.where(mask_c[None], _bmm(q, c_ref[0], split=False), NEG_BIG)
        s_n = jnp.where(mask_n[None], _bmm(q, heads(qkv_ref[A_GROUPS + g, 0]), "nt", split=False), NEG_BIG)
        m = jnp.maximum(jnp.max(s_c, -1, keepdims=True), jnp.max(s_n, -1, keepdims=True))
        p_c = jnp.exp(s_c - m)
        p_n = jnp.exp(s_n - m)
        l = jnp.sum(p_c, -1, keepdims=True) + jnp.sum(p_n, -1, keepdims=True)
        o = (_bmm(p_c, c_ref[1], "nt", split=False)
             + _bmm(p_n, heads(qkv_ref[2 * A_GROUPS + g, 0]), split=False))
        outs.append(o / l)
        lses.append(m + jnp.log(l))
    mx = jnp.maximum(jnp.maximum(lses[0], lses[1]), lses[2])
    ws = [jnp.exp(x - mx) for x in lses]
    merged = (ws[0] * outs[0] + ws[1] * outs[1] + ws[2] * outs[2]) / (ws[0] + ws[1] + ws[2])
    for h in range(A_HEADS):
        o_ref[0, :, h * A_DH:(h + 1) * A_DH] = merged[h]


def _attn_sample(qkv4, caches_t, layer):
    _, N, T, _ = qkv4.shape
    in_specs = [pl.BlockSpec((3 * A_GROUPS, 1, T, A_W), lambda n: (0, n, 0, 0))]
    for g in range(A_GROUPS):
        assert caches_t[g].shape[2:] == (2, A_HEADS, A_DH, A_WINDOWS[g])
        in_specs.append(pl.BlockSpec((None, None, 2, A_HEADS, A_DH, A_WINDOWS[g]),
                                     lambda n: (layer, n, 0, 0, 0, 0)))
    return pl.pallas_call(
        functools.partial(_attn_sample_kernel, n_new=T),
        name="attn_sample",
        grid=(N,),
        in_specs=in_specs,
        out_specs=pl.BlockSpec((1, T, A_W), lambda n: (n, 0, 0)),
        out_shape=jax.ShapeDtypeStruct((N, T, A_W), F32),
        compiler_params=_cparams(("parallel",)),
    )(qkv4, *caches_t)


_DIMS = {"nn": (((1,), (0,)), ((), ())), "nt": (((1,), (1,)), ((), ())), "tn": (((0,), (0,)), ((), ()))}


def _mm(a, b, mode="nn"):
    dn = _DIMS[mode]
    a_hi = a.astype(BF16)
    a_lo = (a - a_hi.astype(F32)).astype(BF16)
    b_hi = b.astype(BF16)
    b_lo = (b - b_hi.astype(F32)).astype(BF16)
    out = lax.dot_general(a_lo, b_hi, dn, preferred_element_type=F32)
    out = out + lax.dot_general(a_hi, b_lo, dn, preferred_element_type=F32)
    return out + lax.dot_general(a_hi, b_hi, dn, preferred_element_type=F32)


def _tri_incl(c):
    t = lax.broadcasted_iota(jnp.int32, (c, c), 0)
    s = lax.broadcasted_iota(jnp.int32, (c, c), 1)
    return t, s


SOLVE_BLOCK = 8


_BDIMS = {"nn": (((2,), (1,)), ((0,), (0,))), "nt": (((2,), (2,)), ((0,), (0,))), "tn": (((1,), (1,)), ((0,), (0,)))}


def _bmm(a, b, mode="nn", split=True):
    dn = _BDIMS[mode]
    a_hi = a.astype(BF16)
    if not split:
        return lax.dot_general(a_hi, b.astype(BF16), dn, preferred_element_type=F32)
    a_lo = (a - a_hi.astype(F32)).astype(BF16)
    b_hi = b.astype(BF16)
    b_lo = (b - b_hi.astype(F32)).astype(BF16)
    out = lax.dot_general(a_lo, b_hi, dn, preferred_element_type=F32)
    out = out + lax.dot_general(a_hi, b_lo, dn, preferred_element_type=F32)
    return out + lax.dot_general(a_hi, b_hi, dn, preferred_element_type=F32)


def _solve_unit_lower(lm, z, C):
    blk = min(SOLVE_BLOCK, C)
    done = []
    for j in range(C // blk):
        r0 = j * blk
        zj = z[:, r0:r0 + blk]
        if j > 0:
            zj = zj - _bmm(lm[:, r0:r0 + blk, :r0], jnp.concatenate(done, 1), split=False)
        ljj = lm[:, r0:r0 + blk, r0:r0 + blk]
        for s in range(blk - 1):
            zj = zj - ljj[:, :, s:s + 1] * zj[:, s:s + 1, :]
        done.append(zj)
    return jnp.concatenate(done, 1)


def _rwkv_kernel(bc_ref, bprev_ref, s0_ref, mu_ref, w0_ref, a0_ref, kkp_ref, ka_ref, w2_ref, a2_ref, g2_ref,
                 rk_ref, lng_ref, lnb_ref, o_ref, sout_ref,
                 prev_scr, s_scr, h_gam, h_rt, h_kt, h_v, h_kk, h_ebp, h_aenb, h_rk, h_g, h_y, *, C, nb):
    c = pl.program_id(1)
    H, E = B_HEADS, B_DH
    R = nb * C
    B = nb * H

    @pl.when(c == 0)
    def _():
        prev_scr[...] = bprev_ref[...]
        s_scr[...] = s0_ref[...].reshape(B, E, E)

    bc3 = bc_ref[...]
    bc = bc3.reshape(R, B_COLS)
    step = lax.broadcasted_iota(jnp.int32, (R, B_COLS), 0) % C
    carried = jnp.broadcast_to(prev_scr[...], (nb, C, B_COLS)).reshape(R, B_COLS)
    prev = jnp.where(step == 0, carried, pltpu.roll(bc, 1, 0))
    prev_scr[...] = bc3[:, C - 1:C, :]
    mixed = bc + mu_ref[...] * (prev - bc)
    r = mixed[:, 0:B_W]
    k = mixed[:, B_W:2 * B_W]
    v = mixed[:, 2 * B_W:3 * B_W]
    o1 = 3 * B_W
    wl = mixed[:, o1:o1 + B_W_LORA]
    al = mixed[:, o1 + B_W_LORA:o1 + B_W_LORA + B_A_LORA]
    gl = mixed[:, o1 + B_W_LORA + B_A_LORA:]
    w_raw = -_softplus(-(w0_ref[...] + _mm(jnp.tanh(wl), w2_ref[...]))) - 0.5
    lw = -jnp.exp(w_raw)
    a = _sigmoid(a0_ref[...] + _mm(al, a2_ref[...]))
    gate = _mm(_sigmoid(gl), g2_ref[...])
    kkr = k * kkp_ref[...]
    kb = k * (1.0 + (a - 1.0) * ka_ref[...])
    t_r, s_r = _tri_incl(R)
    b = _dot_exact_lhs((s_r <= t_r) & ((s_r // C) == (t_r // C)), lw)
    eb = jnp.exp(b)
    enb = jnp.exp(-b)
    parts = (r * eb, kb * enb, v, kkr, jnp.exp(b - lw), a * enb, r * kb * rk_ref[...], gate)
    for h in range(H):
        sl = slice(h * E, (h + 1) * E)
        for dst, src in zip((h_rt, h_kt, h_v, h_kk, h_ebp, h_aenb, h_rk, h_g), parts):
            dst[:, h] = src[:, sl].reshape(nb, C, E)
        h_gam[:, h] = eb[:, sl].reshape(nb, C, E)[:, C - 1:C, :]

    def batch(ref):
        return ref[...].reshape((B,) + ref.shape[2:])

    t_i, s_i = _tri_incl(C)
    strict = (s_i < t_i)[None]
    incl = (s_i <= t_i)[None]
    eye = (lax.broadcasted_iota(jnp.int32, (E, E), 0) == lax.broadcasted_iota(jnp.int32, (E, E), 1))[None]
    rt, kt, v3, gam = batch(h_rt), batch(h_kt), batch(h_v), batch(h_gam)
    kk = batch(h_kk)
    kk = kk * lax.rsqrt(jnp.maximum(jnp.sum(kk * kk, -1, keepdims=True), 1e-24))
    at = kk * batch(h_ebp)
    bt = kk * batch(h_aenb)
    x = _bmm(jnp.concatenate([at, rt], 1), jnp.concatenate([bt, kt], 1), "nt", split=False)
    lm = jnp.where(strict, x[:, :C, :C], 0.0)
    mk = jnp.where(strict, x[:, :C, C:], 0.0)
    arb = jnp.where(incl, x[:, C:, :C], 0.0)
    ark = jnp.where(incl, x[:, C:, C:], 0.0)
    z = _solve_unit_lower(lm, jnp.concatenate([at, _bmm(mk, v3, split=False)], 2), C)
    qy = jnp.concatenate([rt, _bmm(ark, v3, split=False)], 2) - _bmm(arb, z, split=False)
    top = jnp.where(eye, gam, 0.0)
    gd = (jnp.concatenate([top, _bmm(v3, kt * gam, "tn", split=False)], 1)
          - _bmm(z, bt * gam, "tn", split=False))
    s_all = s_scr[...]
    y = _bmm(qy[:, :, :E], s_all, "nt", split=False) + qy[:, :, E:]
    s_scr[...] = _bmm(s_all, gd[:, :E]) + gd[:, E:]
    mu_y = jnp.mean(y, -1, keepdims=True)
    var = jnp.mean(jnp.square(y - mu_y), -1, keepdims=True)
    lng = jnp.concatenate([lng_ref[...]] * nb, 0)
    lnb = jnp.concatenate([lnb_ref[...]] * nb, 0)
    yn = (y - mu_y) * lax.rsqrt(var + B_GN_EPS) * lng + lnb
    bonus = jnp.sum(batch(h_rk), -1, keepdims=True) * v3
    h_y[...] = ((yn + bonus) * batch(h_g)).reshape(nb, H, C, E)
    for h in range(H):
        o_ref[:, :, h * E:(h + 1) * E] = h_y[:, h]

    @pl.when(c == pl.num_programs(1) - 1)
    def _():
        sout_ref[...] = s_scr[...].reshape(nb, H, E, E)


def _rwkv(bcols, b_prev, s0, p, C, nb):
    N, L, _ = bcols.shape
    H, E = B_HEADS, B_DH
    assert N % nb == 0 and L % C == 0
    row = lambda a: a.reshape(1, -1)
    per_head = lambda a: a.reshape(H, 1, E)
    full = lambda shape: pl.BlockSpec(shape, lambda n, c: (0,) * len(shape))
    hs = lambda: pltpu.VMEM((nb, H, C, E), F32)
    return pl.pallas_call(
        functools.partial(_rwkv_kernel, C=C, nb=nb),
        name="rwkv",
        grid=(N // nb, L // C),
        in_specs=[pl.BlockSpec((nb, C, B_COLS), lambda n, c: (n, c, 0)),
                  pl.BlockSpec((nb, 1, B_COLS), lambda n, c: (n, 0, 0)),
                  pl.BlockSpec((nb, H, E, E), lambda n, c: (n, 0, 0, 0)),
                  full((1, B_COLS)), full((1, B_W)), full((1, B_W)), full((1, B_W)), full((1, B_W)),
                  full((B_W_LORA, B_W)), full((B_A_LORA, B_W)), full((B_G_LORA, B_W)),
                  full((1, B_W)), full((H, 1, E)), full((H, 1, E))],
        out_specs=[pl.BlockSpec((nb, C, B_W), lambda n, c: (n, c, 0)),
                   pl.BlockSpec((nb, H, E, E), lambda n, c: (n, 0, 0, 0))],
        out_shape=[jax.ShapeDtypeStruct((N, L, B_W), F32), jax.ShapeDtypeStruct((N, H, E, E), F32)],
        scratch_shapes=[pltpu.VMEM((nb, 1, B_COLS), F32), pltpu.VMEM((nb * H, E, E), F32),
                        pltpu.VMEM((nb, H, 1, E), F32)] + [hs() for _ in range(9)],
        compiler_params=_cparams(("parallel", "arbitrary")),
    )(bcols, b_prev.reshape(N, 1, B_COLS), s0, row(p['b_mu']), row(p['b_w0']), row(p['b_a0']), row(p['b_kk']),
      row(p['b_ka']), p['b_w2'], p['b_a2'], p['b_g2'], row(p['b_rk']), per_head(p['b_lnx_g']),
      per_head(p['b_lnx_b']))


def _hgrn_levels(C):
    return int(math.log2(C))


def _hgrn_kernel(ch_ref, h0_ref, lb_ref, cng_ref, o_ref, hout_ref,
                 h_scr, hq, hk, hi, hb, hgt, hbp, ho, *, C, nb):
    c = pl.program_id(1)
    H, E = C_HEADS, C_DK
    n_lv = _hgrn_levels(C)
    R = nb * C
    B = nb * H

    @pl.when(c == 0)
    def _():
        h_scr[...] = h0_ref[...].reshape(B, E, E)

    ch = ch_ref[...].reshape(R, 4 * C_W)
    cq = ch[:, 0:C_W]
    z = ch[:, C_W:2 * C_W]
    ci = ch[:, 2 * C_W:3 * C_W]
    cg = ch[:, 3 * C_W:4 * C_W]
    lb = lb_ref[...]
    loglb = jnp.log(jnp.maximum(lb, C_LB_FLOOR))
    log_f = -_softplus(-z) + _softplus(loglb - z)
    kc = (1.0 - lb) * _sigmoid(-z)
    qc = cq * _sigmoid(cq)
    gt = cg * _sigmoid(cg)
    t_r, s_r = _tri_incl(R)
    b = _dot_exact_lhs((s_r <= t_r) & ((s_r // C) == (t_r // C)), log_f)
    bps = []
    for lv in range(n_lv):
        m = 1 << lv
        sel = s_r == (t_r - (t_r % (2 * m)) + m - 1)
        bps.append(_dot_exact_lhs(sel, b))
    for h in range(H):
        sl = slice(h * E, (h + 1) * E)
        for dst, src in zip((hq, hk, hi, hb, hgt), (qc, kc, ci, b, gt)):
            dst[:, h] = src[:, sl].reshape(nb, C, E)
        for lv in range(n_lv):
            hbp[lv, :, h] = bps[lv][:, sl].reshape(nb, C, E)

    def batch(x):
        return x.reshape((B,) + x.shape[2:])

    t_i, s_i = _tri_incl(C)
    eye = (lax.broadcasted_iota(jnp.int32, (E, E), 0) == lax.broadcasted_iota(jnp.int32, (E, E), 1))[None]
    q3, k3, i3, b3 = batch(hq[...]), batch(hk[...]), batch(hi[...]), batch(hb[...])
    att = jnp.where((t_i == s_i)[None], _bmm(q3, k3, "nt", split=False), 0.0)
    for lv in range(n_lv):
        m = 1 << lv
        e = jnp.exp(-jnp.abs(b3 - batch(hbp[lv])))
        keep = (((t_i // (2 * m)) == (s_i // (2 * m))) & ((t_i % (2 * m)) >= m) & ((s_i % (2 * m)) < m))[None]
        att = att + jnp.where(keep, _bmm(q3 * e, k3 * e, "nt", split=False), 0.0)
    h_m = h_scr[...]
    o = _bmm(q3 * jnp.exp(b3), h_m, split=False) + _bmm(att, i3, split=False)
    b_last = b3[:, C - 1:C, :]
    dg = jnp.where(eye, jnp.exp(b_last), 0.0)
    kdec = k3 * jnp.exp(b_last - b3)
    h_scr[...] = _bmm(jnp.concatenate([dg, kdec], 1), jnp.concatenate([h_m, i3], 1), "tn")
    o = o * lax.rsqrt(jnp.mean(jnp.square(o), -1, keepdims=True) + RMS_EPS) * cng_ref[...]
    ho[...] = (o * batch(hgt[...])).reshape(nb, H, C, E)
    for h in range(H):
        o_ref[:, :, h * E:(h + 1) * E] = ho[:, h]

    @pl.when(c == pl.num_programs(1) - 1)
    def _():
        hout_ref[...] = h_scr[...].reshape(nb, H, E, E)


def _hgrn(ch, h0, lb, c_norm_g, C, nb):
    N, L, _ = ch.shape
    H, E = C_HEADS, C_DK
    n_lv = _hgrn_levels(C)
    assert N % nb == 0 and L % C == 0
    full = lambda shape: pl.BlockSpec(shape, lambda n, c: (0,) * len(shape))
    hs = lambda: pltpu.VMEM((nb, H, C, E), F32)
    return pl.pallas_call(
        functools.partial(_hgrn_kernel, C=C, nb=nb),
        name="hgrn",
        grid=(N // nb, L // C),
        in_specs=[pl.BlockSpec((nb, C, 4 * C_W), lambda n, c: (n, c, 0)),
                  pl.BlockSpec((nb, H, E, E), lambda n, c: (n, 0, 0, 0)),
                  full((1, C_W)), full((1, E))],
        out_specs=[pl.BlockSpec((nb, C, C_W), lambda n, c: (n, c, 0)),
                   pl.BlockSpec((nb, H, E, E), lambda n, c: (n, 0, 0, 0))],
        out_shape=[jax.ShapeDtypeStruct((N, L, C_W), F32), jax.ShapeDtypeStruct((N, H, E, E), F32)],
        scratch_shapes=[pltpu.VMEM((nb * H, E, E), F32), hs(), hs(), hs(), hs(), hs(),
                        pltpu.VMEM((n_lv, nb, H, C, E), F32), hs()],
        compiler_params=_cparams(("parallel", "arbitrary")),
    )(ch, h0, lb.reshape(1, C_W), c_norm_g.reshape(1, E))


def _layer_norm(v, g, b):
    mu = jnp.mean(v, -1, keepdims=True)
    var = jnp.mean(jnp.square(v - mu), -1, keepdims=True)
    return (v - mu) * lax.rsqrt(var + LN_EPS) * g + b


def _merge_kernel(*refs, n_attn, tm):
    attn = refs[:n_attn]
    (ob_ref, oc_ref, gates_ref, x_ref, wa_ref, wb_ref, wc_ref, wo_ref, lng_ref, lnb_ref, wr_ref, br_ref,
     x1_ref, idx_ref, gate_ref, rank_ref, cnt_ref, run_scr) = refs[n_attn:]
    i = pl.program_id(0)

    @pl.when(i == 0)
    def _():
        run_scr[...] = jnp.zeros_like(run_scr)

    if n_attn == 1:
        o_a = attn[0][...]
    else:
        o0, l0, o1, l1, o2, l2 = (r[...] for r in attn)
        mx = jnp.maximum(jnp.maximum(l0, l1), l2)
        w0, w1, w2 = jnp.exp(l0 - mx), jnp.exp(l1 - mx), jnp.exp(l2 - mx)
        o_a = (w0 * o0 + w1 * o1 + w2 * o2) / (w0 + w1 + w2)
    gates = gates_ref[...]
    merged = (_sigmoid(gates[:, 0:D_MODEL]) * _bdot(o_a, wa_ref[...])
              + _sigmoid(gates[:, D_MODEL:2 * D_MODEL]) * _bdot(ob_ref[...], wb_ref[...])
              + _sigmoid(gates[:, 2 * D_MODEL:]) * _bdot(oc_ref[...], wc_ref[...]))
    mix = _bdot(merged, wo_ref[...])
    x1 = _layer_norm(DN_ALPHA * x_ref[...] + mix, lng_ref[...], lnb_ref[...])
    x1_ref[...] = x1

    logits = _bdot(x1, wr_ref[...]) + br_ref[...]
    lane = lax.broadcasted_iota(jnp.int32, (tm, LANES), 1)
    cur = jnp.where(lane < N_EXPERTS, logits, -jnp.inf)
    vals, idxs, sels = [], [], []
    for _ in range(TOP_K):
        mx = jnp.max(cur, -1, keepdims=True)
        idx = jnp.min(jnp.where(cur == mx, lane, LANES), -1, keepdims=True)
        sel = lane == idx
        vals.append(mx)
        idxs.append(idx)
        sels.append(sel)
        cur = jnp.where(sel, -jnp.inf, cur)
    es = [jnp.exp(v - vals[0]) for v in vals]
    den = es[0] + es[1] + es[2] + es[3]
    mask = jnp.zeros((tm, LANES), F32)
    for sel in sels:
        mask = jnp.where(sel, 1.0, mask)
    r_i = lax.broadcasted_iota(jnp.int32, (tm, tm), 0)
    c_i = lax.broadcasted_iota(jnp.int32, (tm, tm), 1)
    excl = jnp.dot((c_i < r_i).astype(BF16), mask.astype(BF16), preferred_element_type=F32) + run_scr[...]
    idx_o = jnp.zeros((tm, LANES), jnp.int32)
    gate_o = jnp.zeros((tm, LANES), F32)
    rank_o = jnp.zeros((tm, LANES), F32)
    for j in range(TOP_K):
        rank_j = jnp.sum(jnp.where(sels[j], excl, 0.0), -1, keepdims=True)
        idx_o = jnp.where(lane == j, idxs[j], idx_o)
        gate_o = jnp.where(lane == j, es[j] / den, gate_o)
        rank_o = jnp.where(lane == j, rank_j, rank_o)
    idx_ref[...] = idx_o
    gate_ref[...] = gate_o
    rank_ref[...] = rank_o.astype(jnp.int32)
    run_scr[...] = run_scr[...] + jnp.sum(mask, axis=0, keepdims=True)
    cnt_ref[...] = jnp.broadcast_to(run_scr[...], cnt_ref.shape)


def _merge(attn, o_b, o_c, gates, x2, wts, tm):
    T = x2.shape[0]
    n_attn = len(attn)
    rowspec = lambda w: pl.BlockSpec((tm, w), lambda i: (i, 0))
    full = lambda a: pl.BlockSpec(a.shape, lambda i: (0,) * a.ndim)
    wlist = [wts['w_br_a'], wts['w_br_b'], wts['w_br_c'], wts['w_out'], wts['ln_g0'], wts['ln_b0'],
             wts['w_router'], wts['b_router']]
    return pl.pallas_call(
        functools.partial(_merge_kernel, n_attn=n_attn, tm=tm),
        name="merge",
        grid=(T // tm,),
        in_specs=[rowspec(A_W)] * n_attn + [rowspec(B_W), rowspec(C_W), rowspec(3 * D_MODEL), rowspec(D_MODEL)]
                 + [full(w) for w in wlist],
        out_specs=[rowspec(D_MODEL), rowspec(LANES), rowspec(LANES), rowspec(LANES),
                   pl.BlockSpec((SUBLANES, LANES), lambda i: (0, 0))],
        out_shape=[jax.ShapeDtypeStruct((T, D_MODEL), F32), jax.ShapeDtypeStruct((T, LANES), jnp.int32),
                   jax.ShapeDtypeStruct((T, LANES), F32), jax.ShapeDtypeStruct((T, LANES), jnp.int32),
                   jax.ShapeDtypeStruct((SUBLANES, LANES), F32)],
        scratch_shapes=[pltpu.VMEM((1, LANES), F32)],
        compiler_params=_cparams(("arbitrary",)),
    )(*attn, o_b, o_c, gates, x2, *wlist)


def _dispatch_kernel(dest_ref, x_ref, xs_in_ref, xs_ref, sem, *, td):
    del xs_in_ref

    n_rows = td * TOP_K

    def issue(t, carry):
        for j in range(TOP_K):
            pltpu.make_async_copy(x_ref.at[pl.ds(t, 1)], xs_ref.at[pl.ds(dest_ref[0, 0, t * TOP_K + j], 1)],
                                  sem).start()
        return carry

    lax.fori_loop(0, td, issue, 0, unroll=2)
    pltpu.make_async_copy(xs_ref.at[pl.ds(0, n_rows)], xs_ref.at[pl.ds(0, n_rows)], sem).wait()


def _dispatch(x1, dest, n_slots, td):
    T = x1.shape[0]
    zeros = jnp.zeros((n_slots, D_MODEL), F32)
    return pl.pallas_call(
        functools.partial(_dispatch_kernel, td=td),
        name="dispatch",
        grid=(T // td,),
        in_specs=[pl.BlockSpec((1, 1, td * TOP_K), lambda i: (i, 0, 0), memory_space=pltpu.SMEM),
                  pl.BlockSpec((td, D_MODEL), lambda i: (i, 0)),
                  pl.BlockSpec(memory_space=pl.ANY)],
        out_specs=pl.BlockSpec(memory_space=pl.ANY),
        out_shape=jax.ShapeDtypeStruct((n_slots, D_MODEL), F32),
        scratch_shapes=[pltpu.SemaphoreType.DMA(())],
        input_output_aliases={2: 0},
        compiler_params=_cparams(("arbitrary",)),
    )(dest.reshape(T // td, 1, td * TOP_K), x1, zeros)


def _expert_kernel(be_ref, nu_ref, xs_ref, wu_ref, bu_ref, wd_ref, bd_ref, o_ref):
    b = pl.program_id(0)

    @pl.when(b < nu_ref[0])
    def _():
        h = jnp.dot(xs_ref[...].astype(BF16), wu_ref[0], preferred_element_type=F32) + bu_ref[0]
        h_glu = jnp.minimum(h[:, :D_FF], SWIGLU_LIMIT)
        h_lin = jnp.clip(h[:, D_FF:], -SWIGLU_LIMIT, SWIGLU_LIMIT)
        act = h_glu * _sigmoid(SWIGLU_ALPHA * h_glu) * (h_lin + 1.0)
        o_ref[...] = jnp.dot(act.astype(BF16), wd_ref[0], preferred_element_type=F32) + bd_ref[0]

    @pl.when(b >= nu_ref[0])
    def _():
        o_ref[...] = jnp.zeros_like(o_ref)


def _experts(xs, block_expert, n_used, w_up, b_up, w_down, b_down):
    n_slots = xs.shape[0]
    n_blocks = n_slots // MOE_BM
    grid_spec = pltpu.PrefetchScalarGridSpec(
        num_scalar_prefetch=2,
        grid=(n_blocks,),
        in_specs=[pl.BlockSpec((MOE_BM, D_MODEL), lambda b, be, nu: (b, 0)),
                  pl.BlockSpec((1, D_MODEL, 2 * D_FF), lambda b, be, nu: (be[b], 0, 0)),
                  pl.BlockSpec((1, 1, 2 * D_FF), lambda b, be, nu: (be[b], 0, 0)),
                  pl.BlockSpec((1, D_FF, D_MODEL), lambda b, be, nu: (be[b], 0, 0)),
                  pl.BlockSpec((1, 1, D_MODEL), lambda b, be, nu: (be[b], 0, 0))],
        out_specs=pl.BlockSpec((MOE_BM, D_MODEL), lambda b, be, nu: (b, 0)),
    )
    return pl.pallas_call(
        _expert_kernel,
        name="experts",
        grid_spec=grid_spec,
        out_shape=jax.ShapeDtypeStruct((n_slots, D_MODEL), F32),
        compiler_params=_cparams(("arbitrary",)),
    )(block_expert, n_used, xs, w_up, b_up.reshape(N_EXPERTS, 1, 2 * D_FF), w_down,
      b_down.reshape(N_EXPERTS, 1, D_MODEL))


def _combine_kernel(dest_ref, gate_ref, x1_ref, eo_ref, lng_ref, lnb_ref, o_ref, buf, sem, *, tc):
    n_rows = tc * TOP_K

    def issue(t, carry):
        for j in range(TOP_K):
            pltpu.make_async_copy(eo_ref.at[pl.ds(dest_ref[0, 0, t * TOP_K + j], 1)],
                                  buf.at[pl.ds(j * tc + t, 1)], sem).start()
        return carry

    lax.fori_loop(0, tc, issue, 0, unroll=2)
    pltpu.make_async_copy(eo_ref.at[pl.ds(0, n_rows)], buf, sem).wait()
    gate = gate_ref[...]
    y = gate[:, 0:1] * buf[0:tc]
    for j in range(1, TOP_K):
        y = y + gate[:, j:j + 1] * buf[j * tc:(j + 1) * tc]
    o_ref[...] = _layer_norm(DN_ALPHA * x1_ref[...] + y, lng_ref[...], lnb_ref[...])


def _combine(dest, gate, x1, eo, ln_g, ln_b, tc):
    T = x1.shape[0]
    full = lambda a: pl.BlockSpec(a.shape, lambda i: (0,) * a.ndim)
    return pl.pallas_call(
        functools.partial(_combine_kernel, tc=tc),
        name="combine",
        grid=(T // tc,),
        in_specs=[pl.BlockSpec((1, 1, tc * TOP_K), lambda i: (i, 0, 0), memory_space=pltpu.SMEM),
                  pl.BlockSpec((tc, LANES), lambda i: (i, 0)),
                  pl.BlockSpec((tc, D_MODEL), lambda i: (i, 0)),
                  pl.BlockSpec(memory_space=pl.ANY), full(ln_g), full(ln_b)],
        out_specs=pl.BlockSpec((tc, D_MODEL), lambda i: (i, 0)),
        out_shape=jax.ShapeDtypeStruct((T, D_MODEL), F32),
        scratch_shapes=[pltpu.VMEM((TOP_K * tc, D_MODEL), F32), pltpu.SemaphoreType.DMA(())],
        compiler_params=_cparams(("arbitrary",)),
    )(dest.reshape(T // tc, 1, tc * TOP_K), gate, x1, eo, ln_g, ln_b)


def _moe(x1, top_idx, top_gate, top_rank, counts, wts):
    T = x1.shape[0]
    cnt = counts[0, :N_EXPERTS].astype(jnp.int32)
    padded = (cnt + MOE_BM - 1) // MOE_BM * MOE_BM
    pend = jnp.cumsum(padded)
    pstart = pend - padded
    dest = (pstart[top_idx[:, :TOP_K]] + top_rank[:, :TOP_K]).astype(jnp.int32)
    n_blocks = T * TOP_K // MOE_BM + N_EXPERTS
    block_start = jnp.arange(n_blocks, dtype=jnp.int32) * MOE_BM
    block_expert = jnp.minimum(jnp.sum(pend[None, :] <= block_start[:, None], axis=1), N_EXPERTS - 1).astype(jnp.int32)
    n_used = (pend[-1:] // MOE_BM).astype(jnp.int32)
    xs = _dispatch(x1, dest, n_blocks * MOE_BM, min(256, T))
    eo = _experts(xs, block_expert, n_used, wts['w_up'], wts['b_up'], wts['w_down'], wts['b_down'])
    return _combine(dest, top_gate, x1, eo, wts['ln_g1'], wts['ln_b1'], min(128, T))


def _hybrid_layer(x, pos, caches, layer, b_shift, b_wkv, c_st, lb, wts):
    N, L, D = x.shape
    T = N * L
    x2 = x.reshape(T, D)
    tm = min(1024, T)
    rows = max(L, tm)
    cos_t, sin_t = _rope_tables(pos, rows)
    qkv = _project_qkv(x2, wts['w_qkv'], cos_t, sin_t, tm, rows // tm)
    bcols = _project(x2, wts['w_b'], tm, B_COLS // 2).reshape(N, L, B_COLS)
    ch = _project(x2, wts['w_c'], tm, 2 * C_W).reshape(N, L, 4 * C_W)
    gates = _project(x2, wts['w_g'], tm, 3 * D_MODEL // 2)

    a_new = []
    if caches is None:
        attn = []
        for g in range(A_GROUPS):
            attn += list(_attn_dilated(qkv, g, N, L))
            a_new.append(_window_rows(qkv, g, N, L))
    else:
        qkv5 = qkv.reshape(3 * A_GROUPS, N, L, A_HEADS, A_DH)
        attn = [_attn_sample(qkv.reshape(3 * A_GROUPS, N, L, A_W), caches, layer).reshape(T, A_W)]
        for g in range(A_GROUPS):
            a_new.append(jnp.stack([qkv5[A_GROUPS + g], qkv5[2 * A_GROUPS + g]], axis=2))

    seqs_per_step = max(1, min(N, RWKV_CHUNK // L))
    o_b, b_s = _rwkv(bcols, b_shift, b_wkv, wts, min(RWKV_CHUNK, L), seqs_per_step)
    o_c, c_s = _hgrn(ch, c_st, lb, wts['c_norm_g'], min(HGRN_CHUNK, L), seqs_per_step)
    x1, top_idx, top_gate, top_rank, counts = _merge(
        attn, o_b.reshape(T, B_W), o_c.reshape(T, C_W), gates, x2, wts, min(512, T))
    x_out = _moe(x1, top_idx, top_gate, top_rank, counts, wts)
    return x_out.reshape(N, L, D), (a_new, bcols[:, -1], b_s, c_s)


def kernel(x_prompt, x_sample, cache_a0_kv, cache_a1_kv, cache_a2_kv, state_b_shift, state_b_wkv, state_c,
           w_in, b_mu, b_w0, b_w2, b_a0, b_a2, b_g2, b_kk, b_ka, b_rk, b_lnx_g, b_lnx_b, c_lb, c_norm_g,
           w_br_a, w_br_b, w_br_c, w_out, ln_g, ln_b, w_router, b_router, w_up, b_up, w_down, b_down):
    lb_all = jax.nn.softmax(c_lb.astype(F32), axis=0)
    lb_all = jnp.cumsum(lb_all, axis=0) - lb_all[0]
    pad_e = LANES - N_EXPERTS
    layers = []
    for l in range(DEPTH):
        layers.append({
            'w_qkv': w_in[l, :, :OFF_B].astype(BF16), 'w_b': w_in[l, :, OFF_B:OFF_C].astype(BF16),
            'w_c': w_in[l, :, OFF_C:OFF_G].astype(BF16), 'w_g': w_in[l, :, OFF_G:].astype(BF16), 'b_mu': b_mu[l], 'b_w0': b_w0[l], 'b_w2': b_w2[l], 'b_a0': b_a0[l],
            'b_a2': b_a2[l], 'b_g2': b_g2[l], 'b_kk': b_kk[l], 'b_ka': b_ka[l], 'b_rk': b_rk[l],
            'b_lnx_g': b_lnx_g[l], 'b_lnx_b': b_lnx_b[l], 'c_norm_g': c_norm_g[l],
            'w_br_a': w_br_a[l].astype(BF16), 'w_br_b': w_br_b[l].astype(BF16), 'w_br_c': w_br_c[l].astype(BF16),
            'w_out': w_out[l].astype(BF16),
            'ln_g0': ln_g[l, 0].reshape(1, D_MODEL), 'ln_b0': ln_b[l, 0].reshape(1, D_MODEL),
            'ln_g1': ln_g[l, 1].reshape(1, D_MODEL), 'ln_b1': ln_b[l, 1].reshape(1, D_MODEL),
            'w_router': jnp.pad(w_router[l].astype(F32), ((0, 0), (0, pad_e))),
            'b_router': jnp.pad(b_router[l].astype(F32), (0, pad_e)).reshape(1, LANES),
            'w_up': w_up[l].astype(BF16), 'b_up': b_up[l], 'w_down': w_down[l].astype(BF16), 'b_down': b_down[l]})

    def run_group(x, pos, caches, b_shift, b_wkv, c_st):
        new = ([], [], [], [], [], [])
        for l in range(DEPTH):
            x, (a_new, b_last, b_s, c_s) = _hybrid_layer(x, pos, caches, l, b_shift[l], b_wkv[l], c_st[l],
                                                         lb_all[l], layers[l])
            for g in range(A_GROUPS):
                new[g].append(a_new[g])
            new[3].append(b_last)
            new[4].append(b_s)
            new[5].append(c_s)
        return x, [jnp.stack(s) for s in new]

    nb_p, seq = x_prompt.shape[0], x_prompt.shape[1]
    y_prompt, (p_a0, p_a1, p_a2, p_bs, p_bw, p_c) = run_group(
        x_prompt, jnp.arange(seq, dtype=jnp.int32), None,
        jnp.zeros((DEPTH, nb_p, B_COLS), F32),
        jnp.zeros((DEPTH, nb_p, B_HEADS, B_DH, B_DH), F32),
        jnp.zeros((DEPTH, nb_p, C_HEADS, C_DK, C_DV), F32))
    p_a0, p_a1, p_a2 = (jnp.transpose(a, (0, 1, 5, 2, 3, 4)) for a in (p_a0, p_a1, p_a2))
    pos_s = PAST_LEN + jnp.arange(x_sample.shape[1], dtype=jnp.int32)
    caches_t = tuple(jnp.transpose(c, (0, 1, 3, 4, 5, 2)) for c in (cache_a0_kv, cache_a1_kv, cache_a2_kv))
    y_sample, (s_a0, s_a1, s_a2, s_bs, s_bw, s_c) = run_group(
        x_sample, pos_s, caches_t, state_b_shift, state_b_wkv, state_c)
    return (y_prompt, y_sample, p_a0, p_a1, p_a2, p_bs, p_bw, p_c, s_a0, s_a1, s_a2, s_bs, s_bw, s_c)
```

```python
import functools
import math

import numpy as np
import jax
import jax.numpy as jnp
from jax import lax
from jax.experimental import pallas as pl
from jax.experimental.pallas import tpu as pltpu

F32 = jnp.float32
BF16 = jnp.bfloat16

D_MODEL = 1024
DEPTH = 2
PAST_LEN = 2048
A_WINDOWS = (128, 512, 2048)
A_DILATIONS = (1, 4, 16)
A_GROUPS = 3
A_HEADS = 8
A_DH = 64
A_W = A_HEADS * A_DH
A_BLK = 128
B_DH = 64
B_HEADS = 16
B_W = 1024
B_W_LORA = 64
B_A_LORA = 64
B_G_LORA = 128
B_COLS = 3 * B_W + B_W_LORA + B_A_LORA + B_G_LORA
B_GN_EPS = 64e-5
C_DK = 64
C_DV = 64
C_HEADS = 16
C_W = 1024
C_LB_FLOOR = 1e-30
N_EXPERTS = 32
TOP_K = 4
D_FF = 1024
SWIGLU_ALPHA = 1.702
SWIGLU_LIMIT = 7.0
ROPE_THETA = 10000.0
LN_EPS = 1e-5
RMS_EPS = 1e-6
NEG_BIG = -1e30
DN_ALPHA = (2 * DEPTH) ** 0.25
QKV_W = 3 * A_GROUPS * A_W
OFF_B = QKV_W
OFF_C = OFF_B + B_COLS
OFF_G = OFF_C + 4 * C_W
N_IN = OFF_G + 3 * D_MODEL

LANES = 128
SUBLANES = 8
MXU_N = 256
VMEM_LIMIT = 56 * 1024 * 1024

RWKV_CHUNK = 64
HGRN_CHUNK = 64
MOE_BM = 256


def _cparams(sem):
    return pltpu.CompilerParams(dimension_semantics=sem, vmem_limit_bytes=VMEM_LIMIT)


def _bdot(a, b):
    return jnp.dot(a.astype(BF16), b.astype(BF16), preferred_element_type=F32)


def _bdot_nt(a, b):
    return lax.dot_general(a.astype(BF16), b.astype(BF16), (((1,), (1,)), ((), ())), preferred_element_type=F32)


def _bdot_tn(a, b):
    return lax.dot_general(a.astype(BF16), b.astype(BF16), (((0,), (0,)), ((), ())), preferred_element_type=F32)


def _split3(x):
    hi = x.astype(BF16)
    r1 = x - hi.astype(F32)
    mid = r1.astype(BF16)
    lo = (r1 - mid.astype(F32)).astype(BF16)
    return hi, mid, lo


def _dot_exact_lhs(sel, x):
    sel = sel.astype(BF16)
    hi, mid, lo = _split3(x)
    out = jnp.dot(sel, lo, preferred_element_type=F32)
    out = out + jnp.dot(sel, mid, preferred_element_type=F32)
    return out + jnp.dot(sel, hi, preferred_element_type=F32)


def _dot3(a, b):
    a_hi = a.astype(BF16)
    a_lo = (a - a_hi.astype(F32)).astype(BF16)
    b_hi = b.astype(BF16)
    b_lo = (b - b_hi.astype(F32)).astype(BF16)
    out = jnp.dot(a_lo, b_hi, preferred_element_type=F32)
    out = out + jnp.dot(a_hi, b_lo, preferred_element_type=F32)
    return out + jnp.dot(a_hi, b_hi, preferred_element_type=F32)


def _sigmoid(x):
    return 1.0 / (1.0 + jnp.exp(-x))


def _softplus(x):
    return jnp.maximum(x, 0.0) + jnp.log(1.0 + jnp.exp(-jnp.abs(x)))


def _proj_kernel(x_ref, w_ref, o_ref, xb_scr):
    @pl.when(pl.program_id(1) == 0)
    def _():
        xb_scr[...] = x_ref[...].astype(BF16)

    o_ref[...] = jnp.dot(xb_scr[...], w_ref[...], preferred_element_type=F32)


def _project(x2, w_seg, tm, tn):
    T, D = x2.shape
    n_cols = w_seg.shape[1]
    assert n_cols % tn == 0 and tn % LANES == 0 and T % tm == 0
    return pl.pallas_call(
        _proj_kernel,
        name="proj",
        grid=(T // tm, n_cols // tn),
        in_specs=[pl.BlockSpec((tm, D), lambda i, j: (i, 0)),
                  pl.BlockSpec((D, tn), lambda i, j: (0, j))],
        out_specs=pl.BlockSpec((tm, tn), lambda i, j: (i, j)),
        out_shape=jax.ShapeDtypeStruct((T, n_cols), F32),
        scratch_shapes=[pltpu.VMEM((tm, D), BF16)],
        compiler_params=_cparams(("parallel", "arbitrary")),
    )(x2, w_seg)


def _qkv_rope_kernel(x_ref, w_ref, cos_ref, sin_ref, o_ref, xb_scr):
    j = pl.program_id(1)

    @pl.when(j == 0)
    def _():
        xb_scr[...] = x_ref[...].astype(BF16)

    acc = jnp.dot(xb_scr[...], w_ref[...], preferred_element_type=F32)

    @pl.when(j < 2 * A_GROUPS)
    def _():
        lane = lax.broadcasted_iota(jnp.int32, acc.shape, 1)
        first = (lane % A_DH) < (A_DH // 2)
        rot = jnp.where(first, pltpu.roll(acc, A_W - A_DH // 2, 1), pltpu.roll(acc, A_DH // 2, 1))
        o_ref[0] = acc * cos_ref[...] + rot * sin_ref[...]

    @pl.when(j >= 2 * A_GROUPS)
    def _():
        o_ref[0] = acc


def _project_qkv(x2, w_bf16, cos_t, sin_t, tm, period_blocks):
    T, D = x2.shape
    return pl.pallas_call(
        _qkv_rope_kernel,
        name="qkv_rope",
        grid=(T // tm, 3 * A_GROUPS),
        in_specs=[pl.BlockSpec((tm, D), lambda i, j: (i, 0)),
                  pl.BlockSpec((D, A_W), lambda i, j: (0, j)),
                  pl.BlockSpec((tm, A_W), lambda i, j: (i % period_blocks, 0)),
                  pl.BlockSpec((tm, A_W), lambda i, j: (i % period_blocks, 0))],
        out_specs=pl.BlockSpec((1, tm, A_W), lambda i, j: (j, i, 0)),
        out_shape=jax.ShapeDtypeStruct((3 * A_GROUPS, T, A_W), F32),
        scratch_shapes=[pltpu.VMEM((tm, D), BF16)],
        compiler_params=_cparams(("parallel", "arbitrary")),
    )(x2, w_bf16, cos_t, sin_t)


def _rope_tables(pos, rows):
    half = A_DH // 2
    inv = ROPE_THETA ** (-jnp.arange(half, dtype=F32) / half)
    ang = pos.astype(F32)[:, None] * inv[None, :]
    cos = jnp.cos(ang)
    sin = jnp.sin(ang)
    cos_h = jnp.concatenate([cos, cos], -1)
    sin_h = jnp.concatenate([-sin, sin], -1)
    cos_f = jnp.tile(cos_h, (rows // pos.shape[0], A_HEADS))
    sin_f = jnp.tile(sin_h, (rows // pos.shape[0], A_HEADS))
    return cos_f, sin_f


def _attn_dilated_kernel(q_ref, kc_ref, kp_ref, vc_ref, vp_ref, o_ref, lse_ref, *, d, m):
    jb = pl.program_id(1)
    two = 2 * A_BLK
    span = A_BLK * d
    r2 = lax.broadcasted_iota(jnp.int32, (two, LANES), 0)
    l2 = lax.broadcasted_iota(jnp.int32, (two, LANES), 1)
    qi = r2 % A_BLK
    own_half = (r2 < A_BLK) == (l2 < A_DH)
    mask_c = l2 <= qi
    mask_p = l2 >= qi
    first = lax.broadcasted_iota(jnp.int32, (A_BLK, LANES), 1) < A_DH

    def rows(i, r):
        return pl.ds(i * span + r, A_BLK, stride=d) if d > 1 else pl.ds(i * span, A_BLK)

    for i in range(m):
        for r in range(d):
            cur = rows(i, r)
            q = q_ref[cur, :] * (A_DH ** -0.5)
            q2 = jnp.where(own_half, jnp.concatenate([q, q], axis=0), 0.0)
            if i == 0:
                k_prev, v_prev = kp_ref[rows(0, r), :], vp_ref[rows(0, r), :]
                keep_p = mask_p & (jb > 0)
            else:
                k_prev, v_prev = kc_ref[rows(i - 1, r), :], vc_ref[rows(i - 1, r), :]
                keep_p = mask_p
            s_c = jnp.where(mask_c, _bdot_nt(q2, kc_ref[cur, :]), NEG_BIG)
            s_p = jnp.where(keep_p, _bdot_nt(q2, k_prev), NEG_BIG)
            mx = jnp.maximum(jnp.max(s_c, -1, keepdims=True), jnp.max(s_p, -1, keepdims=True))
            p_c = jnp.exp(s_c - mx)
            p_p = jnp.exp(s_p - mx)
            l = jnp.sum(p_c, -1, keepdims=True) + jnp.sum(p_p, -1, keepdims=True)
            o2 = (_bdot(p_c, vc_ref[cur, :]) + _bdot(p_p, v_prev)) / l
            lse2 = jnp.broadcast_to(mx + jnp.log(l), (two, LANES))
            o_ref[cur, :] = jnp.where(first, o2[:A_BLK], o2[A_BLK:])
            lse_ref[cur, :] = jnp.where(first, lse2[:A_BLK], lse2[A_BLK:])


ATTN_ROWS_PER_STEP = 512


def _attn_dilated(qkv, g, N, L):
    d = A_DILATIONS[g]
    T = N * L
    span = A_BLK * d
    m = max(1, ATTN_ROWS_PER_STEP // span)
    blk = span * m
    assert L % blk == 0
    nblk = L // blk
    cur = lambda a: pl.BlockSpec((None, blk, LANES), lambda n, j, p: (a, n * nblk + j, p))
    prv = lambda a: pl.BlockSpec((None, span, LANES),
                                 lambda n, j, p: (a, n * nblk * m + jnp.maximum(j * m - 1, 0), p))
    out = pl.BlockSpec((blk, LANES), lambda n, j, p: (n * nblk + j, p))
    shp = jax.ShapeDtypeStruct((T, A_W), F32)
    return pl.pallas_call(
        functools.partial(_attn_dilated_kernel, d=d, m=m),
        name="attn_dilated",
        grid=(N, nblk, A_HEADS // 2),
        in_specs=[cur(g), cur(A_GROUPS + g), prv(A_GROUPS + g), cur(2 * A_GROUPS + g), prv(2 * A_GROUPS + g)],
        out_specs=[out, out],
        out_shape=[shp, shp],
        compiler_params=_cparams(("parallel", "arbitrary", "arbitrary")),
    )(qkv, qkv, qkv, qkv, qkv)


def _window_rows_kernel(x_ref, o_ref):
    xt = x_ref[0].T
    for h in range(A_HEADS):
        o_ref[0, 0, h] = xt[h * A_DH:(h + 1) * A_DH, :]


def _window_rows(qkv, g, N, L):
    keep = min(A_WINDOWS[g], L)
    tw = min(keep, 512)
    assert (L - keep) % tw == 0
    first = lambda n: n * (L // tw) + (L - keep) // tw
    return pl.pallas_call(
        _window_rows_kernel,
        name="window_rows",
        grid=(N, 2, keep // tw),
        in_specs=[pl.BlockSpec((1, tw, A_W), lambda n, kv, j: (A_GROUPS * (1 + kv) + g, first(n) + j, 0))],
        out_specs=pl.BlockSpec((1, 1, A_HEADS, A_DH, tw), lambda n, kv, j: (n, kv, 0, 0, j)),
        out_shape=jax.ShapeDtypeStruct((N, 2, A_HEADS, A_DH, keep), F32),
        compiler_params=_cparams(("parallel", "parallel", "arbitrary")),
    )(qkv)


def _attn_sample_kernel(qkv_ref, c0_ref, c1_ref, c2_ref, o_ref, *, n_new):
    caches = (c0_ref, c1_ref, c2_ref)
    T = n_new
    masks = []
    for g in range(A_GROUPS):
        d = A_DILATIONS[g]
        W = A_WINDOWS[g]
        back_c = lax.broadcasted_iota(jnp.int32, (T, W), 0) - lax.broadcasted_iota(jnp.int32, (T, W), 1)
        back_n = lax.broadcasted_iota(jnp.int32, (T, T), 0) - lax.broadcasted_iota(jnp.int32, (T, T), 1)
        masks.append(((back_c <= 0) & ((back_c & (d - 1)) == 0), (back_n >= 0) & ((back_n & (d - 1)) == 0)))
    def heads(rows):
        return jnp.stack([rows[:, h * A_DH:(h + 1) * A_DH] for h in range(A_HEADS)], 0)

    outs, lses = [], []
    for g in range(A_GROUPS):
        c_ref = caches[g]
        mask_c, mask_n = masks[g]
        q = heads(qkv_ref[g, 0]) * (A_DH ** -0.5)
        s_c = jnp.where(mask_c[None], _bmm(q, c_ref[0], split=False), NEG_BIG)
        s_n = jnp.where(mask_n[None], _bmm(q, heads(qkv_ref[A_GROUPS + g, 0]), "nt", split=False), NEG_BIG)
        m = jnp.maximum(jnp.max(s_c, -1, keepdims=True), jnp.max(s_n, -1, keepdims=True))
        p_c = jnp.exp(s_c - m)
        p_n = jnp.exp(s_n - m)
        l = jnp.sum(p_c, -1, keepdims=True) + jnp.sum(p_n, -1, keepdims=True)
        o = (_bmm(p_c, c_ref[1], "nt", split=False)
             + _bmm(p_n, heads(qkv_ref[2 * A_GROUPS + g, 0]), split=False))
        outs.append(o / l)
        lses.append(m + jnp.log(l))
    mx = jnp.maximum(jnp.maximum(lses[0], lses[1]), lses[2])
    ws = [jnp.exp(x - mx) for x in lses]
    merged = (ws[0] * outs[0] + ws[1] * outs[1] + ws[2] * outs[2]) / (ws[0] + ws[1] + ws[2])
    for h in range(A_HEADS):
        o_ref[0, :, h * A_DH:(h + 1) * A_DH] = merged[h]


def _attn_sample(qkv4, caches_t, layer):
    _, N, T, _ = qkv4.shape
    in_specs = [pl.BlockSpec((3 * A_GROUPS, 1, T, A_W), lambda n: (0, n, 0, 0))]
    for g in range(A_GROUPS):
        assert caches_t[g].shape[2:] == (2, A_HEADS, A_DH, A_WINDOWS[g])
        in_specs.append(pl.BlockSpec((None, None, 2, A_HEADS, A_DH, A_WINDOWS[g]),
                                     lambda n: (layer, n, 0, 0, 0, 0)))
    return pl.pallas_call(
        functools.partial(_attn_sample_kernel, n_new=T),
        name="attn_sample",
        grid=(N,),
        in_specs=in_specs,
        out_specs=pl.BlockSpec((1, T, A_W), lambda n: (n, 0, 0)),
        out_shape=jax.ShapeDtypeStruct((N, T, A_W), F32),
        compiler_params=_cparams(("parallel",)),
    )(qkv4, *caches_t)


_DIMS = {"nn": (((1,), (0,)), ((), ())), "nt": (((1,), (1,)), ((), ())), "tn": (((0,), (0,)), ((), ()))}


def _mm(a, b, mode="nn"):
    dn = _DIMS[mode]
    a_hi = a.astype(BF16)
    a_lo = (a - a_hi.astype(F32)).astype(BF16)
    b_hi = b.astype(BF16)
    b_lo = (b - b_hi.astype(F32)).astype(BF16)
    out = lax.dot_general(a_lo, b_hi, dn, preferred_element_type=F32)
    out = out + lax.dot_general(a_hi, b_lo, dn, preferred_element_type=F32)
    return out + lax.dot_general(a_hi, b_hi, dn, preferred_element_type=F32)


def _tri_incl(c):
    t = lax.broadcasted_iota(jnp.int32, (c, c), 0)
    s = lax.broadcasted_iota(jnp.int32, (c, c), 1)
    return t, s


SOLVE_BLOCK = 8


_BDIMS = {"nn": (((2,), (1,)), ((0,), (0,))), "nt": (((2,), (2,)), ((0,), (0,))), "tn": (((1,), (1,)), ((0,), (0,)))}


def _bmm(a, b, mode="nn", split=True):
    dn = _BDIMS[mode]
    a_hi = a.astype(BF16)
    if not split:
        return lax.dot_general(a_hi, b.astype(BF16), dn, preferred_element_type=F32)
    a_lo = (a - a_hi.astype(F32)).astype(BF16)
    b_hi = b.astype(BF16)
    b_lo = (b - b_hi.astype(F32)).astype(BF16)
    out = lax.dot_general(a_lo, b_hi, dn, preferred_element_type=F32)
    out = out + lax.dot_general(a_hi, b_lo, dn, preferred_element_type=F32)
    return out + lax.dot_general(a_hi, b_hi, dn, preferred_element_type=F32)


def _solve_unit_lower(lm, z, C):
    blk = min(SOLVE_BLOCK, C)
    done = []
    for j in range(C // blk):
        r0 = j * blk
        zj = z[:, r0:r0 + blk]
        if j > 0:
            zj = zj - _bmm(lm[:, r0:r0 + blk, :r0], jnp.concatenate(done, 1), split=False)
        ljj = lm[:, r0:r0 + blk, r0:r0 + blk]
        for s in range(blk - 1):
            zj = zj - ljj[:, :, s:s + 1] * zj[:, s:s + 1, :]
        done.append(zj)
    return jnp.concatenate(done, 1)


def _rwkv_kernel(bc_ref, bprev_ref, s0_ref, mu_ref, w0_ref, a0_ref, kkp_ref, ka_ref, w2_ref, a2_ref, g2_ref,
                 rk_ref, lng_ref, lnb_ref, o_ref, sout_ref,
                 prev_scr, s_scr, h_gam, h_rt, h_kt, h_v, h_kk, h_ebp, h_aenb, h_rk, h_g, h_y, *, C, nb):
    c = pl.program_id(1)
    H, E = B_HEADS, B_DH
    R = nb * C
    B = nb * H

    @pl.when(c == 0)
    def _():
        prev_scr[...] = bprev_ref[...]
        s_scr[...] = s0_ref[...].reshape(B, E, E)

    bc3 = bc_ref[...]
    bc = bc3.reshape(R, B_COLS)
    step = lax.broadcasted_iota(jnp.int32, (R, B_COLS), 0) % C
    carried = jnp.broadcast_to(prev_scr[...], (nb, C, B_COLS)).reshape(R, B_COLS)
    prev = jnp.where(step == 0, carried, pltpu.roll(bc, 1, 0))
    prev_scr[...] = bc3[:, C - 1:C, :]
    mixed = bc + mu_ref[...] * (prev - bc)
    r = mixed[:, 0:B_W]
    k = mixed[:, B_W:2 * B_W]
    v = mixed[:, 2 * B_W:3 * B_W]
    o1 = 3 * B_W
    wl = mixed[:, o1:o1 + B_W_LORA]
    al = mixed[:, o1 + B_W_LORA:o1 + B_W_LORA + B_A_LORA]
    gl = mixed[:, o1 + B_W_LORA + B_A_LORA:]
    w_raw = -_softplus(-(w0_ref[...] + _mm(jnp.tanh(wl), w2_ref[...]))) - 0.5
    lw = -jnp.exp(w_raw)
    a = _sigmoid(a0_ref[...] + _mm(al, a2_ref[...]))
    gate = _mm(_sigmoid(gl), g2_ref[...])
    kkr = k * kkp_ref[...]
    kb = k * (1.0 + (a - 1.0) * ka_ref[...])
    t_r, s_r = _tri_incl(R)
    b = _dot_exact_lhs((s_r <= t_r) & ((s_r // C) == (t_r // C)), lw)
    eb = jnp.exp(b)
    enb = jnp.exp(-b)
    parts = (r * eb, kb * enb, v, kkr, jnp.exp(b - lw), a * enb, r * kb * rk_ref[...], gate)
    for h in range(H):
        sl = slice(h * E, (h + 1) * E)
        for dst, src in zip((h_rt, h_kt, h_v, h_kk, h_ebp, h_aenb, h_rk, h_g), parts):
            dst[:, h] = src[:, sl].reshape(nb, C, E)
        h_gam[:, h] = eb[:, sl].reshape(nb, C, E)[:, C - 1:C, :]

    def batch(ref):
        return ref[...].reshape((B,) + ref.shape[2:])

    t_i, s_i = _tri_incl(C)
    strict = (s_i < t_i)[None]
    incl = (s_i <= t_i)[None]
    eye = (lax.broadcasted_iota(jnp.int32, (E, E), 0) == lax.broadcasted_iota(jnp.int32, (E, E), 1))[None]
    rt, kt, v3, gam = batch(h_rt), batch(h_kt), batch(h_v), batch(h_gam)
    kk = batch(h_kk)
    kk = kk * lax.rsqrt(jnp.maximum(jnp.sum(kk * kk, -1, keepdims=True), 1e-24))
    at = kk * batch(h_ebp)
    bt = kk * batch(h_aenb)
    x = _bmm(jnp.concatenate([at, rt], 1), jnp.concatenate([bt, kt], 1), "nt", split=False)
    lm = jnp.where(strict, x[:, :C, :C], 0.0)
    mk = jnp.where(strict, x[:, :C, C:], 0.0)
    arb = jnp.where(incl, x[:, C:, :C], 0.0)
    ark = jnp.where(incl, x[:, C:, C:], 0.0)
    z = _solve_unit_lower(lm, jnp.concatenate([at, _bmm(mk, v3, split=False)], 2), C)
    qy = jnp.concatenate([rt, _bmm(ark, v3, split=False)], 2) - _bmm(arb, z, split=False)
    top = jnp.where(eye, gam, 0.0)
    gd = (jnp.concatenate([top, _bmm(v3, kt * gam, "tn", split=False)], 1)
          - _bmm(z, bt * gam, "tn", split=False))
    s_all = s_scr[...]
    y = _bmm(qy[:, :, :E], s_all, "nt", split=False) + qy[:, :, E:]
    s_scr[...] = _bmm(s_all, gd[:, :E]) + gd[:, E:]
    mu_y = jnp.mean(y, -1, keepdims=True)
    var = jnp.mean(jnp.square(y - mu_y), -1, keepdims=True)
    lng = jnp.concatenate([lng_ref[...]] * nb, 0)
    lnb = jnp.concatenate([lnb_ref[...]] * nb, 0)
    yn = (y - mu_y) * lax.rsqrt(var + B_GN_EPS) * lng + lnb
    bonus = jnp.sum(batch(h_rk), -1, keepdims=True) * v3
    h_y[...] = ((yn + bonus) * batch(h_g)).reshape(nb, H, C, E)
    for h in range(H):
        o_ref[:, :, h * E:(h + 1) * E] = h_y[:, h]

    @pl.when(c == pl.num_programs(1) - 1)
    def _():
        sout_ref[...] = s_scr[...].reshape(nb, H, E, E)


def _rwkv(bcols, b_prev, s0, p, C, nb):
    N, L, _ = bcols.shape
    H, E = B_HEADS, B_DH
    assert N % nb == 0 and L % C == 0
    row = lambda a: a.reshape(1, -1)
    per_head = lambda a: a.reshape(H, 1, E)
    full = lambda shape: pl.BlockSpec(shape, lambda n, c: (0,) * len(shape))
    hs = lambda: pltpu.VMEM((nb, H, C, E), F32)
    return pl.pallas_call(
        functools.partial(_rwkv_kernel, C=C, nb=nb),
        name="rwkv",
        grid=(N // nb, L // C),
        in_specs=[pl.BlockSpec((nb, C, B_COLS), lambda n, c: (n, c, 0)),
                  pl.BlockSpec((nb, 1, B_COLS), lambda n, c: (n, 0, 0)),
                  pl.BlockSpec((nb, H, E, E), lambda n, c: (n, 0, 0, 0)),
                  full((1, B_COLS)), full((1, B_W)), full((1, B_W)), full((1, B_W)), full((1, B_W)),
                  full((B_W_LORA, B_W)), full((B_A_LORA, B_W)), full((B_G_LORA, B_W)),
                  full((1, B_W)), full((H, 1, E)), full((H, 1, E))],
        out_specs=[pl.BlockSpec((nb, C, B_W), lambda n, c: (n, c, 0)),
                   pl.BlockSpec((nb, H, E, E), lambda n, c: (n, 0, 0, 0))],
        out_shape=[jax.ShapeDtypeStruct((N, L, B_W), F32), jax.ShapeDtypeStruct((N, H, E, E), F32)],
        scratch_shapes=[pltpu.VMEM((nb, 1, B_COLS), F32), pltpu.VMEM((nb * H, E, E), F32),
                        pltpu.VMEM((nb, H, 1, E), F32)] + [hs() for _ in range(9)],
        compiler_params=_cparams(("parallel", "arbitrary")),
    )(bcols, b_prev.reshape(N, 1, B_COLS), s0, row(p['b_mu']), row(p['b_w0']), row(p['b_a0']), row(p['b_kk']),
      row(p['b_ka']), p['b_w2'], p['b_a2'], p['b_g2'], row(p['b_rk']), per_head(p['b_lnx_g']),
      per_head(p['b_lnx_b']))


def _hgrn_levels(C):
    return int(math.log2(C))


def _hgrn_kernel(ch_ref, h0_ref, lb_ref, cng_ref, o_ref, hout_ref,
                 h_scr, hq, hk, hi, hb, hgt, hbp, ho, *, C, nb):
    c = pl.program_id(1)
    H, E = C_HEADS, C_DK
    n_lv = _hgrn_levels(C)
    R = nb * C
    B = nb * H

    @pl.when(c == 0)
    def _():
        h_scr[...] = h0_ref[...].reshape(B, E, E)

    ch = ch_ref[...].reshape(R, 4 * C_W)
    cq = ch[:, 0:C_W]
    z = ch[:, C_W:2 * C_W]
    ci = ch[:, 2 * C_W:3 * C_W]
    cg = ch[:, 3 * C_W:4 * C_W]
    lb = lb_ref[...]
    loglb = jnp.log(jnp.maximum(lb, C_LB_FLOOR))
    log_f = -_softplus(-z) + _softplus(loglb - z)
    kc = (1.0 - lb) * _sigmoid(-z)
    qc = cq * _sigmoid(cq)
    gt = cg * _sigmoid(cg)
    t_r, s_r = _tri_incl(R)
    b = _dot_exact_lhs((s_r <= t_r) & ((s_r // C) == (t_r // C)), log_f)
    bps = []
    for lv in range(n_lv):
        m = 1 << lv
        sel = s_r == (t_r - (t_r % (2 * m)) + m - 1)
        bps.append(_dot_exact_lhs(sel, b))
    for h in range(H):
        sl = slice(h * E, (h + 1) * E)
        for dst, src in zip((hq, hk, hi, hb, hgt), (qc, kc, ci, b, gt)):
            dst[:, h] = src[:, sl].reshape(nb, C, E)
        for lv in range(n_lv):
            hbp[lv, :, h] = bps[lv][:, sl].reshape(nb, C, E)

    def batch(x):
        return x.reshape((B,) + x.shape[2:])

    t_i, s_i = _tri_incl(C)
    eye = (lax.broadcasted_iota(jnp.int32, (E, E), 0) == lax.broadcasted_iota(jnp.int32, (E, E), 1))[None]
    q3, k3, i3, b3 = batch(hq[...]), batch(hk[...]), batch(hi[...]), batch(hb[...])
    att = jnp.where((t_i == s_i)[None], _bmm(q3, k3, "nt", split=False), 0.0)
    for lv in range(n_lv):
        m = 1 << lv
        e = jnp.exp(-jnp.abs(b3 - batch(hbp[lv])))
        keep = (((t_i // (2 * m)) == (s_i // (2 * m))) & ((t_i % (2 * m)) >= m) & ((s_i % (2 * m)) < m))[None]
        att = att + jnp.where(keep, _bmm(q3 * e, k3 * e, "nt", split=False), 0.0)
    h_m = h_scr[...]
    o = _bmm(q3 * jnp.exp(b3), h_m, split=False) + _bmm(att, i3, split=False)
    b_last = b3[:, C - 1:C, :]
    dg = jnp.where(eye, jnp.exp(b_last), 0.0)
    kdec = k3 * jnp.exp(b_last - b3)
    h_scr[...] = _bmm(jnp.concatenate([dg, kdec], 1), jnp.concatenate([h_m, i3], 1), "tn")
    o = o * lax.rsqrt(jnp.mean(jnp.square(o), -1, keepdims=True) + RMS_EPS) * cng_ref[...]
    ho[...] = (o * batch(hgt[...])).reshape(nb, H, C, E)
    for h in range(H):
        o_ref[:, :, h * E:(h + 1) * E] = ho[:, h]

    @pl.when(c == pl.num_programs(1) - 1)
    def _():
        hout_ref[...] = h_scr[...].reshape(nb, H, E, E)


def _hgrn(ch, h0, lb, c_norm_g, C, nb):
    N, L, _ = ch.shape
    H, E = C_HEADS, C_DK
    n_lv = _hgrn_levels(C)
    assert N % nb == 0 and L % C == 0
    full = lambda shape: pl.BlockSpec(shape, lambda n, c: (0,) * len(shape))
    hs = lambda: pltpu.VMEM((nb, H, C, E), F32)
    return pl.pallas_call(
        functools.partial(_hgrn_kernel, C=C, nb=nb),
        name="hgrn",
        grid=(N // nb, L // C),
        in_specs=[pl.BlockSpec((nb, C, 4 * C_W), lambda n, c: (n, c, 0)),
                  pl.BlockSpec((nb, H, E, E), lambda n, c: (n, 0, 0, 0)),
                  full((1, C_W)), full((1, E))],
        out_specs=[pl.BlockSpec((nb, C, C_W), lambda n, c: (n, c, 0)),
                   pl.BlockSpec((nb, H, E, E), lambda n, c: (n, 0, 0, 0))],
        out_shape=[jax.ShapeDtypeStruct((N, L, C_W), F32), jax.ShapeDtypeStruct((N, H, E, E), F32)],
        scratch_shapes=[pltpu.VMEM((nb * H, E, E), F32), hs(), hs(), hs(), hs(), hs(),
                        pltpu.VMEM((n_lv, nb, H, C, E), F32), hs()],
        compiler_params=_cparams(("parallel", "arbitrary")),
    )(ch, h0, lb.reshape(1, C_W), c_norm_g.reshape(1, E))


def _layer_norm(v, g, b):
    mu = jnp.mean(v, -1, keepdims=True)
    var = jnp.mean(jnp.square(v - mu), -1, keepdims=True)
    return (v - mu) * lax.rsqrt(var + LN_EPS) * g + b


def _merge_kernel(*refs, n_attn, tm):
    attn = refs[:n_attn]
    (ob_ref, oc_ref, gates_ref, x_ref, wa_ref, wb_ref, wc_ref, wo_ref, lng_ref, lnb_ref, wr_ref, br_ref,
     x1_ref, idx_ref, gate_ref, rank_ref, cnt_ref, run_scr) = refs[n_attn:]
    i = pl.program_id(0)

    @pl.when(i == 0)
    def _():
        run_scr[...] = jnp.zeros_like(run_scr)

    if n_attn == 1:
        o_a = attn[0][...]
    else:
        o0, l0, o1, l1, o2, l2 = (r[...] for r in attn)
        mx = jnp.maximum(jnp.maximum(l0, l1), l2)
        w0, w1, w2 = jnp.exp(l0 - mx), jnp.exp(l1 - mx), jnp.exp(l2 - mx)
        o_a = (w0 * o0 + w1 * o1 + w2 * o2) / (w0 + w1 + w2)
    gates = gates_ref[...]
    merged = (_sigmoid(gates[:, 0:D_MODEL]) * _bdot(o_a, wa_ref[...])
              + _sigmoid(gates[:, D_MODEL:2 * D_MODEL]) * _bdot(ob_ref[...], wb_ref[...])
              + _sigmoid(gates[:, 2 * D_MODEL:]) * _bdot(oc_ref[...], wc_ref[...]))
    mix = _bdot(merged, wo_ref[...])
    x1 = _layer_norm(DN_ALPHA * x_ref[...] + mix, lng_ref[...], lnb_ref[...])
    x1_ref[...] = x1

    logits = _bdot(x1, wr_ref[...]) + br_ref[...]
    lane = lax.broadcasted_iota(jnp.int32, (tm, LANES), 1)
    cur = jnp.where(lane < N_EXPERTS, logits, -jnp.inf)
    vals, idxs, sels = [], [], []
    for _ in range(TOP_K):
        mx = jnp.max(cur, -1, keepdims=True)
        idx = jnp.min(jnp.where(cur == mx, lane, LANES), -1, keepdims=True)
        sel = lane == idx
        vals.append(mx)
        idxs.append(idx)
        sels.append(sel)
        cur = jnp.where(sel, -jnp.inf, cur)
    es = [jnp.exp(v - vals[0]) for v in vals]
    den = es[0] + es[1] + es[2] + es[3]
    mask = jnp.zeros((tm, LANES), F32)
    for sel in sels:
        mask = jnp.where(sel, 1.0, mask)
    r_i = lax.broadcasted_iota(jnp.int32, (tm, tm), 0)
    c_i = lax.broadcasted_iota(jnp.int32, (tm, tm), 1)
    excl = jnp.dot((c_i < r_i).astype(BF16), mask.astype(BF16), preferred_element_type=F32) + run_scr[...]
    idx_o = jnp.zeros((tm, LANES), jnp.int32)
    gate_o = jnp.zeros((tm, LANES), F32)
    rank_o = jnp.zeros((tm, LANES), F32)
    for j in range(TOP_K):
        rank_j = jnp.sum(jnp.where(sels[j], excl, 0.0), -1, keepdims=True)
        idx_o = jnp.where(lane == j, idxs[j], idx_o)
        gate_o = jnp.where(lane == j, es[j] / den, gate_o)
        rank_o = jnp.where(lane == j, rank_j, rank_o)
    idx_ref[...] = idx_o
    gate_ref[...] = gate_o
    rank_ref[...] = rank_o.astype(jnp.int32)
    run_scr[...] = run_scr[...] + jnp.sum(mask, axis=0, keepdims=True)
    cnt_ref[...] = jnp.broadcast_to(run_scr[...], cnt_ref.shape)


def _merge(attn, o_b, o_c, gates, x2, wts, tm):
    T = x2.shape[0]
    n_attn = len(attn)
    rowspec = lambda w: pl.BlockSpec((tm, w), lambda i: (i, 0))
    full = lambda a: pl.BlockSpec(a.shape, lambda i: (0,) * a.ndim)
    wlist = [wts['w_br_a'], wts['w_br_b'], wts['w_br_c'], wts['w_out'], wts['ln_g0'], wts['ln_b0'],
             wts['w_router'], wts['b_router']]
    return pl.pallas_call(
        functools.partial(_merge_kernel, n_attn=n_attn, tm=tm),
        name="merge",
        grid=(T // tm,),
        in_specs=[rowspec(A_W)] * n_attn + [rowspec(B_W), rowspec(C_W), rowspec(3 * D_MODEL), rowspec(D_MODEL)]
                 + [full(w) for w in wlist],
        out_specs=[rowspec(D_MODEL), rowspec(LANES), rowspec(LANES), rowspec(LANES),
                   pl.BlockSpec((SUBLANES, LANES), lambda i: (0, 0))],
        out_shape=[jax.ShapeDtypeStruct((T, D_MODEL), F32), jax.ShapeDtypeStruct((T, LANES), jnp.int32),
                   jax.ShapeDtypeStruct((T, LANES), F32), jax.ShapeDtypeStruct((T, LANES), jnp.int32),
                   jax.ShapeDtypeStruct((SUBLANES, LANES), F32)],
        scratch_shapes=[pltpu.VMEM((1, LANES), F32)],
        compiler_params=_cparams(("arbitrary",)),
    )(*attn, o_b, o_c, gates, x2, *wlist)


def _dispatch_kernel(dest_ref, x_ref, xs_in_ref, xs_ref, sem, *, td):
    del xs_in_ref

    n_rows = td * TOP_K

    def issue(t, carry):
        for j in range(TOP_K):
            pltpu.make_async_copy(x_ref.at[pl.ds(t, 1)], xs_ref.at[pl.ds(dest_ref[0, 0, t * TOP_K + j], 1)],
                                  sem).start()
        return carry

    lax.fori_loop(0, td, issue, 0, unroll=2)
    pltpu.make_async_copy(xs_ref.at[pl.ds(0, n_rows)], xs_ref.at[pl.ds(0, n_rows)], sem).wait()


def _dispatch(x1, dest, n_slots, td):
    T = x1.shape[0]
    zeros = jnp.zeros((n_slots, D_MODEL), F32)
    return pl.pallas_call(
        functools.partial(_dispatch_kernel, td=td),
        name="dispatch",
        grid=(T // td,),
        in_specs=[pl.BlockSpec((1, 1, td * TOP_K), lambda i: (i, 0, 0), memory_space=pltpu.SMEM),
                  pl.BlockSpec((td, D_MODEL), lambda i: (i, 0)),
                  pl.BlockSpec(memory_space=pl.ANY)],
        out_specs=pl.BlockSpec(memory_space=pl.ANY),
        out_shape=jax.ShapeDtypeStruct((n_slots, D_MODEL), F32),
        scratch_shapes=[pltpu.SemaphoreType.DMA(())],
        input_output_aliases={2: 0},
        compiler_params=_cparams(("arbitrary",)),
    )(dest.reshape(T // td, 1, td * TOP_K), x1, zeros)


def _expert_kernel(be_ref, nu_ref, xs_ref, wu_ref, bu_ref, wd_ref, bd_ref, o_ref, wu_bf, wd_bf):
    b = pl.program_id(0)
    used = b < nu_ref[0]
    first_visit = (b == 0) | (be_ref[b] != be_ref[jnp.maximum(b - 1, 0)])

    @pl.when(used & first_visit)
    def _():
        wu_bf[...] = wu_ref[0].astype(BF16)
        wd_bf[...] = wd_ref[0].astype(BF16)

    @pl.when(used)
    def _():
        h = jnp.dot(xs_ref[...].astype(BF16), wu_bf[...], preferred_element_type=F32) + bu_ref[0]
        h_glu = jnp.minimum(h[:, :D_FF], SWIGLU_LIMIT)
        h_lin = jnp.clip(h[:, D_FF:], -SWIGLU_LIMIT, SWIGLU_LIMIT)
        act = h_glu * _sigmoid(SWIGLU_ALPHA * h_glu) * (h_lin + 1.0)
        o_ref[...] = jnp.dot(act.astype(BF16), wd_bf[...], preferred_element_type=F32) + bd_ref[0]

    @pl.when(b >= nu_ref[0])
    def _():
        o_ref[...] = jnp.zeros_like(o_ref)


def _experts(xs, block_expert, n_used, w_up, b_up, w_down, b_down):
    n_slots = xs.shape[0]
    n_blocks = n_slots // MOE_BM
    grid_spec = pltpu.PrefetchScalarGridSpec(
        num_scalar_prefetch=2,
        grid=(n_blocks,),
        in_specs=[pl.BlockSpec((MOE_BM, D_MODEL), lambda b, be, nu: (b, 0)),
                  pl.BlockSpec((1, D_MODEL, 2 * D_FF), lambda b, be, nu: (be[b], 0, 0)),
                  pl.BlockSpec((1, 1, 2 * D_FF), lambda b, be, nu: (be[b], 0, 0)),
                  pl.BlockSpec((1, D_FF, D_MODEL), lambda b, be, nu: (be[b], 0, 0)),
                  pl.BlockSpec((1, 1, D_MODEL), lambda b, be, nu: (be[b], 0, 0))],
        out_specs=pl.BlockSpec((MOE_BM, D_MODEL), lambda b, be, nu: (b, 0)),
        scratch_shapes=[pltpu.VMEM((D_MODEL, 2 * D_FF), BF16), pltpu.VMEM((D_FF, D_MODEL), BF16)],
    )
    return pl.pallas_call(
        _expert_kernel,
        name="experts",
        grid_spec=grid_spec,
        out_shape=jax.ShapeDtypeStruct((n_slots, D_MODEL), F32),
        compiler_params=_cparams(("arbitrary",)),
    )(block_expert, n_used, xs, w_up, b_up.reshape(N_EXPERTS, 1, 2 * D_FF), w_down,
      b_down.reshape(N_EXPERTS, 1, D_MODEL))


def _combine_kernel(dest_ref, gate_ref, x1_ref, eo_ref, lng_ref, lnb_ref, o_ref, buf, sem, *, tc):
    n_rows = tc * TOP_K

    def issue(t, carry):
        for j in range(TOP_K):
            pltpu.make_async_copy(eo_ref.at[pl.ds(dest_ref[0, 0, t * TOP_K + j], 1)],
                                  buf.at[pl.ds(j * tc + t, 1)], sem).start()
        return carry

    lax.fori_loop(0, tc, issue, 0, unroll=2)
    pltpu.make_async_copy(eo_ref.at[pl.ds(0, n_rows)], buf, sem).wait()
    gate = gate_ref[...]
    y = gate[:, 0:1] * buf[0:tc]
    for j in range(1, TOP_K):
        y = y + gate[:, j:j + 1] * buf[j * tc:(j + 1) * tc]
    o_ref[...] = _layer_norm(DN_ALPHA * x1_ref[...] + y, lng_ref[...], lnb_ref[...])


def _combine(dest, gate, x1, eo, ln_g, ln_b, tc):
    T = x1.shape[0]
    full = lambda a: pl.BlockSpec(a.shape, lambda i: (0,) * a.ndim)
    return pl.pallas_call(
        functools.partial(_combine_kernel, tc=tc),
        name="combine",
        grid=(T // tc,),
        in_specs=[pl.BlockSpec((1, 1, tc * TOP_K), lambda i: (i, 0, 0), memory_space=pltpu.SMEM),
                  pl.BlockSpec((tc, LANES), lambda i: (i, 0)),
                  pl.BlockSpec((tc, D_MODEL), lambda i: (i, 0)),
                  pl.BlockSpec(memory_space=pl.ANY), full(ln_g), full(ln_b)],
        out_specs=pl.BlockSpec((tc, D_MODEL), lambda i: (i, 0)),
        out_shape=jax.ShapeDtypeStruct((T, D_MODEL), F32),
        scratch_shapes=[pltpu.VMEM((TOP_K * tc, D_MODEL), F32), pltpu.SemaphoreType.DMA(())],
        compiler_params=_cparams(("arbitrary",)),
    )(dest.reshape(T // tc, 1, tc * TOP_K), gate, x1, eo, ln_g, ln_b)


def _moe(x1, top_idx, top_gate, top_rank, counts, wts):
    T = x1.shape[0]
    cnt = counts[0, :N_EXPERTS].astype(jnp.int32)
    padded = (cnt + MOE_BM - 1) // MOE_BM * MOE_BM
    pend = jnp.cumsum(padded)
    pstart = pend - padded
    dest = (pstart[top_idx[:, :TOP_K]] + top_rank[:, :TOP_K]).astype(jnp.int32)
    n_blocks = T * TOP_K // MOE_BM + N_EXPERTS
    block_start = jnp.arange(n_blocks, dtype=jnp.int32) * MOE_BM
    block_expert = jnp.minimum(jnp.sum(pend[None, :] <= block_start[:, None], axis=1), N_EXPERTS - 1).astype(jnp.int32)
    n_used = (pend[-1:] // MOE_BM).astype(jnp.int32)
    xs = _dispatch(x1, dest, n_blocks * MOE_BM, min(256, T))
    eo = _experts(xs, block_expert, n_used, wts['w_up'], wts['b_up'], wts['w_down'], wts['b_down'])
    return _combine(dest, top_gate, x1, eo, wts['ln_g1'], wts['ln_b1'], min(128, T))


def _hybrid_layer(x, pos, caches, layer, b_shift, b_wkv, c_st, lb, wts):
    N, L, D = x.shape
    T = N * L
    x2 = x.reshape(T, D)
    tm = min(1024, T)
    rows = max(L, tm)
    cos_t, sin_t = _rope_tables(pos, rows)
    qkv = _project_qkv(x2, wts['w_qkv'], cos_t, sin_t, tm, rows // tm)
    bcols = _project(x2, wts['w_b'], tm, B_COLS // 2).reshape(N, L, B_COLS)
    ch = _project(x2, wts['w_c'], tm, 2 * C_W).reshape(N, L, 4 * C_W)
    gates = _project(x2, wts['w_g'], tm, 3 * D_MODEL // 2)

    a_new = []
    if caches is None:
        attn = []
        for g in range(A_GROUPS):
            attn += list(_attn_dilated(qkv, g, N, L))
            a_new.append(_window_rows(qkv, g, N, L))
    else:
        qkv5 = qkv.reshape(3 * A_GROUPS, N, L, A_HEADS, A_DH)
        attn = [_attn_sample(qkv.reshape(3 * A_GROUPS, N, L, A_W), caches, layer).reshape(T, A_W)]
        for g in range(A_GROUPS):
            a_new.append(jnp.stack([qkv5[A_GROUPS + g], qkv5[2 * A_GROUPS + g]], axis=2))

    seqs_per_step = max(1, min(N, RWKV_CHUNK // L))
    o_b, b_s = _rwkv(bcols, b_shift, b_wkv, wts, min(RWKV_CHUNK, L), seqs_per_step)
    o_c, c_s = _hgrn(ch, c_st, lb, wts['c_norm_g'], min(HGRN_CHUNK, L), seqs_per_step)
    x1, top_idx, top_gate, top_rank, counts = _merge(
        attn, o_b.reshape(T, B_W), o_c.reshape(T, C_W), gates, x2, wts, min(512, T))
    x_out = _moe(x1, top_idx, top_gate, top_rank, counts, wts)
    return x_out.reshape(N, L, D), (a_new, bcols[:, -1], b_s, c_s)


def kernel(x_prompt, x_sample, cache_a0_kv, cache_a1_kv, cache_a2_kv, state_b_shift, state_b_wkv, state_c,
           w_in, b_mu, b_w0, b_w2, b_a0, b_a2, b_g2, b_kk, b_ka, b_rk, b_lnx_g, b_lnx_b, c_lb, c_norm_g,
           w_br_a, w_br_b, w_br_c, w_out, ln_g, ln_b, w_router, b_router, w_up, b_up, w_down, b_down):
    lb_all = jax.nn.softmax(c_lb.astype(F32), axis=0)
    lb_all = jnp.cumsum(lb_all, axis=0) - lb_all[0]
    pad_e = LANES - N_EXPERTS
    layers = []
    for l in range(DEPTH):
        layers.append({
            'w_qkv': w_in[l, :, :OFF_B].astype(BF16), 'w_b': w_in[l, :, OFF_B:OFF_C].astype(BF16),
            'w_c': w_in[l, :, OFF_C:OFF_G].astype(BF16), 'w_g': w_in[l, :, OFF_G:].astype(BF16), 'b_mu': b_mu[l], 'b_w0': b_w0[l], 'b_w2': b_w2[l], 'b_a0': b_a0[l],
            'b_a2': b_a2[l], 'b_g2': b_g2[l], 'b_kk': b_kk[l], 'b_ka': b_ka[l], 'b_rk': b_rk[l],
            'b_lnx_g': b_lnx_g[l], 'b_lnx_b': b_lnx_b[l], 'c_norm_g': c_norm_g[l],
            'w_br_a': w_br_a[l].astype(BF16), 'w_br_b': w_br_b[l].astype(BF16), 'w_br_c': w_br_c[l].astype(BF16),
            'w_out': w_out[l].astype(BF16),
            'ln_g0': ln_g[l, 0].reshape(1, D_MODEL), 'ln_b0': ln_b[l, 0].reshape(1, D_MODEL),
            'ln_g1': ln_g[l, 1].reshape(1, D_MODEL), 'ln_b1': ln_b[l, 1].reshape(1, D_MODEL),
            'w_router': jnp.pad(w_router[l].astype(F32), ((0, 0), (0, pad_e))),
            'b_router': jnp.pad(b_router[l].astype(F32), (0, pad_e)).reshape(1, LANES),
            'w_up': w_up[l], 'b_up': b_up[l], 'w_down': w_down[l], 'b_down': b_down[l]})

    def run_group(x, pos, caches, b_shift, b_wkv, c_st):
        new = ([], [], [], [], [], [])
        for l in range(DEPTH):
            x, (a_new, b_last, b_s, c_s) = _hybrid_layer(x, pos, caches, l, b_shift[l], b_wkv[l], c_st[l],
                                                         lb_all[l], layers[l])
            for g in range(A_GROUPS):
                new[g].append(a_new[g])
            new[3].append(b_last)
            new[4].append(b_s)
            new[5].append(c_s)
        return x, [jnp.stack(s) for s in new]

    nb_p, seq = x_prompt.shape[0], x_prompt.shape[1]
    y_prompt, (p_a0, p_a1, p_a2, p_bs, p_bw, p_c) = run_group(
        x_prompt, jnp.arange(seq, dtype=jnp.int32), None,
        jnp.zeros((DEPTH, nb_p, B_COLS), F32),
        jnp.zeros((DEPTH, nb_p, B_HEADS, B_DH, B_DH), F32),
        jnp.zeros((DEPTH, nb_p, C_HEADS, C_DK, C_DV), F32))
    p_a0, p_a1, p_a2 = (jnp.transpose(a, (0, 1, 5, 2, 3, 4)) for a in (p_a0, p_a1, p_a2))
    pos_s = PAST_LEN + jnp.arange(x_sample.shape[1], dtype=jnp.int32)
    caches_t = tuple(jnp.transpose(c, (0, 1, 3, 4, 5, 2)) for c in (cache_a0_kv, cache_a1_kv, cache_a2_kv))
    y_sample, (s_a0, s_a1, s_a2, s_bs, s_bw, s_c) = run_group(
        x_sample, pos_s, caches_t, state_b_shift, state_b_wkv, state_c)
    return (y_prompt, y_sample, p_a0, p_a1, p_a2, p_bs, p_bw, p_c, s_a0, s_a1, s_a2, s_bs, s_bw, s_c)
```
